```python
import math
import jax, jax.numpy as jnp
from jax import lax
import numpy as np

D_MODEL = 1024
BATCH = 2
SEQ = 8192
DEPTH = 2
DEC_BATCH = 32
DEC_SEQ = 4
PAST_LEN = 16384
PAGE_SIZE = 128

N_MIXERS = 2
N_ATTN_LAYERS = (DEPTH + 1) // 2
N_HGRN_LAYERS = DEPTH // 2
N_HEADS = 8
HEAD_DIM = D_MODEL // N_HEADS
MOBA_BLOCK = 256
MOBA_TOPK = 3
Q_BLOCK = 128
N_BUCKETS = 32
MAX_DISTANCE = 128
HG_EXPAND = 128
HG_HEADS = D_MODEL // HG_EXPAND
HG_KEY = HG_EXPAND
HG_VAL = D_MODEL // HG_HEADS
HG_CHUNK = 64
D_FF = 4 * D_MODEL
EPS = 1e-6

kernel_name = 'moba_hgrn2_hybrid_decode_step'


def rmsnorm(x, g):
    xf = x.astype(jnp.float32)
    y = xf * lax.rsqrt(jnp.mean(xf * xf, axis=-1, keepdims=True) + EPS)
    return (y * g.astype(jnp.float32)).astype(x.dtype)


def t5_bucket(dist):
    n = jnp.maximum(dist, 0)
    max_exact = N_BUCKETS // 2
    nf = jnp.maximum(n, 1).astype(jnp.float32)
    large = max_exact + (jnp.log(nf / max_exact) / math.log(MAX_DISTANCE / max_exact)
                         * (N_BUCKETS - max_exact)).astype(jnp.int32)
    large = jnp.minimum(large, N_BUCKETS - 1)
    return jnp.where(n < max_exact, n, large)


def sqrelu_mlp(xn, w1, w2):
    h = jax.nn.relu(jnp.einsum('bsd,df->bsf', xn, w1))
    return jnp.einsum('bsf,fd->bsd', h * h, w2)


def moba_prompt(xn, w_qkv, w_o, rel_bias):
    B, S, _ = xn.shape
    qkv = jnp.einsum('bsd,de->bse', xn, w_qkv).reshape(B, S, 3, N_HEADS, HEAD_DIM)
    q, k, v = qkv[:, :, 0], qkv[:, :, 1], qkv[:, :, 2]
    nb = -(-S // MOBA_BLOCK)
    pad = nb * MOBA_BLOCK - S

    def to_blocks(a):
        a = jnp.pad(a, ((0, 0), (0, pad), (0, 0), (0, 0)))
        return a.reshape(B, nb, MOBA_BLOCK, N_HEADS, HEAD_DIM).transpose(0, 3, 1, 2, 4)

    kb, vb = to_blocks(k), to_blocks(v)
    k_mean = jnp.mean(kb, axis=3, dtype=jnp.float32)
    qh = q.transpose(0, 2, 1, 3)
    gate = jnp.einsum('bhsd,bhnd->bhsn', qh.astype(jnp.float32), k_mean)
    q_blk = jnp.arange(S) // MOBA_BLOCK
    fully_past = jnp.arange(nb)[None, :] < q_blk[:, None]
    gate = jnp.where(fully_past, gate, -jnp.inf)
    k_sel = min(MOBA_TOPK, nb)
    _, sel = lax.top_k(gate, k_sel)

    nqb = S // Q_BLOCK
    qh_b = qh.reshape(B, N_HEADS, nqb, Q_BLOCK, HEAD_DIM).transpose(2, 0, 1, 3, 4)
    sel_b = sel.reshape(B, N_HEADS, nqb, Q_BLOCK, k_sel).transpose(2, 0, 1, 3, 4)
    b_idx = jnp.arange(B)[:, None, None, None]
    h_idx = jnp.arange(N_HEADS)[None, :, None, None]
    h_idx5 = h_idx[..., None]
    offs = jnp.arange(MOBA_BLOCK)
    rb_t = rel_bias.T.astype(jnp.float32)
    scale = HEAD_DIM ** -0.5
    n_s = k_sel * MOBA_BLOCK

    def one_query_block(args):
        qb, qq, ss = args
        qpos = qb * Q_BLOCK + jnp.arange(Q_BLOCK)
        own = (qb * Q_BLOCK) // MOBA_BLOCK
        k_g = kb[b_idx, h_idx, ss]
        v_g = vb[b_idx, h_idx, ss]
        kpos = ss[..., None] * MOBA_BLOCK + offs
        l_s = jnp.einsum('bhqd,bhqkld->bhqkl', qq, k_g, preferred_element_type=jnp.float32) * scale
        l_s = l_s + rb_t[h_idx5, t5_bucket(qpos[:, None, None] - kpos)]
        valid = (ss < (qpos // MOBA_BLOCK)[:, None])[..., None]
        l_s = jnp.where(valid, l_s, -jnp.inf).reshape(B, N_HEADS, Q_BLOCK, n_s)
        k_o = lax.dynamic_index_in_dim(kb, own, axis=2, keepdims=False)
        v_o = lax.dynamic_index_in_dim(vb, own, axis=2, keepdims=False)
        dist = qpos[:, None] - (own * MOBA_BLOCK + offs)[None, :]
        l_o = jnp.einsum('bhqd,bhld->bhql', qq, k_o, preferred_element_type=jnp.float32) * scale
        l_o = jnp.where(dist >= 0, l_o + rb_t[:, t5_bucket(dist)], -jnp.inf)
        p = jax.nn.softmax(jnp.concatenate([l_s, l_o], axis=-1), axis=-1)
        p_s = p[..., :n_s].reshape(B, N_HEADS, Q_BLOCK, k_sel, MOBA_BLOCK).astype(v_g.dtype)
        p_o = p[..., n_s:].astype(v_o.dtype)
        return (jnp.einsum('bhqkl,bhqkld->bhqd', p_s, v_g)
                + jnp.einsum('bhql,bhld->bhqd', p_o, v_o))

    o = lax.map(one_query_block, (jnp.arange(nqb), qh_b, sel_b))
    o = o.transpose(1, 0, 3, 2, 4).reshape(B, S, D_MODEL)
    return jnp.einsum('bsd,de->bse', o, w_o), k, v


def moba_sample(xn, cache_k, cache_v, layer, page_table, w_qkv, w_o, rel_bias):
    DB, T, _ = xn.shape
    n_pages = page_table.shape[1]
    past_len = n_pages * PAGE_SIZE
    ppb = MOBA_BLOCK // PAGE_SIZE
    nb_past = past_len // MOBA_BLOCK
    r_pages = (past_len % MOBA_BLOCK) // PAGE_SIZE
    own_start = nb_past * MOBA_BLOCK
    qkv = jnp.einsum('bsd,de->bse', xn, w_qkv).reshape(DB, T, 3, N_HEADS, HEAD_DIM)
    q, k, v = qkv[:, :, 0], qkv[:, :, 1], qkv[:, :, 2]
    qh = q.transpose(0, 2, 1, 3)
    qpos = past_len + jnp.arange(T)
    rb_t = rel_bias.T.astype(jnp.float32)
    scale = HEAD_DIM ** -0.5
    logit_parts = []
    n_s = 0
    if nb_past > 0:
        k_sel = min(MOBA_TOPK, nb_past)
        n_s = k_sel * MOBA_BLOCK
        page_mean = jnp.mean(cache_k, axis=2, dtype=jnp.float32)[layer]
        blk_mean = page_mean[page_table[:, :nb_past * ppb]].reshape(
            DB, nb_past, ppb, N_HEADS, HEAD_DIM).mean(axis=2)
        gate = jnp.einsum('bhtd,bnhd->bhtn', qh.astype(jnp.float32), blk_mean)
        _, sel = lax.top_k(gate, k_sel)
        log_pages = sel[..., None] * ppb + jnp.arange(ppb)
        phys = page_table[jnp.arange(DB)[:, None, None, None, None], log_pages]
        h6 = jnp.arange(N_HEADS).reshape(1, N_HEADS, 1, 1, 1, 1)
        rows = jnp.arange(PAGE_SIZE)
        k_g = cache_k[layer, phys[..., None], rows, h6].reshape(DB, N_HEADS, T, k_sel, MOBA_BLOCK, HEAD_DIM)
        v_g = cache_v[layer, phys[..., None], rows, h6].reshape(DB, N_HEADS, T, n_s, HEAD_DIM)
        kpos = sel[..., None] * MOBA_BLOCK + jnp.arange(MOBA_BLOCK)
        h5 = jnp.arange(N_HEADS).reshape(1, N_HEADS, 1, 1, 1)
        l_s = jnp.einsum('bhtd,bhtkld->bhtkl', qh, k_g, preferred_element_type=jnp.float32) * scale
        l_s = l_s + rb_t[h5, t5_bucket(qpos[:, None, None] - kpos)]
        logit_parts.append(l_s.reshape(DB, N_HEADS, T, n_s))
    if r_pages > 0:
        phys_o = page_table[:, n_pages - r_pages:]
        k_o = jnp.concatenate([cache_k[layer, phys_o].reshape(DB, r_pages * PAGE_SIZE, N_HEADS, HEAD_DIM), k], axis=1)
        v_o = jnp.concatenate([cache_v[layer, phys_o].reshape(DB, r_pages * PAGE_SIZE, N_HEADS, HEAD_DIM), v], axis=1)
    else:
        k_o, v_o = k, v
    dist = qpos[:, None] - (own_start + jnp.arange(k_o.shape[1]))[None, :]
    l_o = jnp.einsum('bhtd,bshd->bhts', qh, k_o, preferred_element_type=jnp.float32) * scale
    l_o = jnp.where(dist >= 0, l_o + rb_t[:, t5_bucket(dist)], -jnp.inf)
    logit_parts.append(l_o)
    p = jax.nn.softmax(jnp.concatenate(logit_parts, axis=-1), axis=-1)
    o = jnp.einsum('bhts,bshd->bthd', p[..., n_s:].astype(v_o.dtype), v_o)
    if nb_past > 0:
        o = o + jnp.einsum('bhtn,bhtnd->bthd', p[..., :n_s].astype(v_g.dtype), v_g)
    o = o.reshape(DB, T, D_MODEL)
    return jnp.einsum('bsd,de->bse', o, w_o), k, v


def _split_heads(a, n_heads):
    B, S, _ = a.shape
    return a.reshape(B, S, n_heads, -1).transpose(0, 2, 1, 3)


def gla_chunked(q, k, v, logf, s0, chunk):
    B, H, S, dk = q.shape
    dv = v.shape[-1]
    n = S // chunk

    def chunks(a):
        return a.reshape(B, H, n, chunk, a.shape[-1]).transpose(2, 0, 1, 3, 4)

    tri = jnp.tril(jnp.ones((chunk, chunk), dtype=bool))[:, :, None]

    def step(s_prev, inp):
        qc, kc, vc, gc = inp
        b = jnp.cumsum(gc, axis=2)
        o = jnp.einsum('bhtk,bhkv->bhtv', qc * jnp.exp(b), s_prev)
        dec = jnp.exp(jnp.where(tri, b[:, :, :, None, :] - b[:, :, None, :, :], -jnp.inf))
        att = jnp.einsum('bhtk,bhsk,bhtsk->bhts', qc, kc, dec)
        o = o + jnp.einsum('bhts,bhsv->bhtv', att, vc)
        b_last = b[:, :, -1:, :]
        s_new = (jnp.exp(b_last[:, :, 0, :, None]) * s_prev
                 + jnp.einsum('bhsk,bhsv->bhkv', kc * jnp.exp(b_last - b), vc))
        return s_new, o

    s_fin, o = lax.scan(step, s0, (chunks(q), chunks(k), chunks(v), chunks(logf)))
    return o.transpose(1, 2, 0, 3, 4).reshape(B, H, S, dv), s_fin


def hgrn2_mixer(xn, s0, w_in, lb, g_norm, w_o, chunk):
    B, S, _ = xn.shape
    h = jnp.einsum('bsd,de->bse', xn, w_in)
    q_pre, f_pre, i_in, g_pre = jnp.split(h, 4, axis=-1)
    lbf = lb.astype(jnp.float32)
    fgate = lbf + (1.0 - lbf) * jax.nn.sigmoid(f_pre.astype(jnp.float32))
    logf = jnp.log(fgate)
    kk = 1.0 - fgate
    qq = jax.nn.silu(q_pre.astype(jnp.float32)) * (HG_KEY ** -0.5)
    o, s_fin = gla_chunked(_split_heads(qq, HG_HEADS), _split_heads(kk, HG_HEADS),
                           _split_heads(i_in.astype(jnp.float32), HG_HEADS),
                           _split_heads(logf, HG_HEADS), s0.astype(jnp.float32), chunk)
    o = o * lax.rsqrt(jnp.mean(o * o, axis=-1, keepdims=True) + EPS)
    o = o * g_norm.astype(jnp.float32).reshape(HG_HEADS, HG_VAL)[None, :, None, :]
    o = o.transpose(0, 2, 1, 3).reshape(B, S, D_MODEL) * jax.nn.silu(g_pre.astype(jnp.float32))
    y = jnp.einsum('bse,ed->bsd', o.astype(xn.dtype), w_o)
    return y, s_fin.astype(s0.dtype)


def setup_inputs(seed: int = 0) -> dict:
    key = jax.random.key(seed)
    ks = jax.random.split(key, 20)
    f32 = jnp.float32
    n_pages = PAST_LEN // PAGE_SIZE
    n_used = DEC_BATCH * n_pages
    n_pool = n_used + (n_used + 3) // 4

    def nrm(k, shape, scale):
        return scale * jax.random.normal(k, shape, f32)

    perm = jax.random.permutation(ks[5], n_pool)[:n_used]
    return {
        'x_prompt': nrm(ks[0], (BATCH, SEQ, D_MODEL), 1.0),
        'x_sample': nrm(ks[1], (DEC_BATCH, DEC_SEQ, D_MODEL), 1.0),
        'cache_k': nrm(ks[2], (N_ATTN_LAYERS, n_pool, PAGE_SIZE, N_HEADS, HEAD_DIM), 1.0),
        'cache_v': nrm(ks[3], (N_ATTN_LAYERS, n_pool, PAGE_SIZE, N_HEADS, HEAD_DIM), 1.0),
        'state_hgrn': nrm(ks[4], (N_HGRN_LAYERS, DEC_BATCH, HG_HEADS, HG_KEY, HG_VAL), 0.3),
        'page_table': perm.reshape(DEC_BATCH, n_pages).astype(jnp.int32),
        'norm_mix': 1.0 + nrm(ks[6], (DEPTH, D_MODEL), 0.02),
        'norm_ffn': 1.0 + nrm(ks[7], (DEPTH, D_MODEL), 0.02),
        'norm_final': 1.0 + nrm(ks[8], (D_MODEL,), 0.02),
        'attn_w_qkv': nrm(ks[9], (N_ATTN_LAYERS, D_MODEL, 3 * D_MODEL), D_MODEL ** -0.5),
        'attn_w_o': nrm(ks[10], (N_ATTN_LAYERS, D_MODEL, D_MODEL), D_MODEL ** -0.5),
        'rel_bias': nrm(ks[11], (N_BUCKETS, N_HEADS), 0.5),
        'hg_w_in': nrm(ks[12], (N_HGRN_LAYERS, D_MODEL, 4 * D_MODEL), D_MODEL ** -0.5),
        'hg_lb_param': nrm(ks[13], (DEPTH, D_MODEL), 0.5),
        'hg_norm': 1.0 + nrm(ks[14], (N_HGRN_LAYERS, D_MODEL), 0.02),
        'hg_w_o': nrm(ks[15], (N_HGRN_LAYERS, D_MODEL, D_MODEL), D_MODEL ** -0.5),
        'ffn_w1': nrm(ks[16], (DEPTH, D_MODEL, D_FF), D_MODEL ** -0.5),
        'ffn_w2': nrm(ks[17], (DEPTH, D_FF, D_MODEL), D_FF ** -0.5),
    }


def reference(x_prompt, x_sample, cache_k, cache_v, state_hgrn, page_table, norm_mix, norm_ffn,
              norm_final, attn_w_qkv, attn_w_o, rel_bias, hg_w_in, hg_lb_param, hg_norm, hg_w_o,
              ffn_w1, ffn_w2):
    sm = jax.nn.softmax(hg_lb_param.astype(jnp.float32), axis=0)
    lb_all = jnp.cumsum(sm, axis=0)
    lb_all = lb_all - lb_all[0:1]
    xp, xs = x_prompt, x_sample
    kp, vp, ksl, vsl, sp, ssl = [], [], [], [], [], []
    for layer in range(DEPTH):
        hp = rmsnorm(xp, norm_mix[layer])
        hs = rmsnorm(xs, norm_mix[layer])
        if layer % N_MIXERS == 0:
            a = layer // N_MIXERS
            yp, k_p, v_p = moba_prompt(hp, attn_w_qkv[a], attn_w_o[a], rel_bias)
            ys, k_s, v_s = moba_sample(hs, cache_k, cache_v, a, page_table, attn_w_qkv[a], attn_w_o[a], rel_bias)
            kp.append(k_p); vp.append(v_p); ksl.append(k_s); vsl.append(v_s)
        else:
            l = layer // N_MIXERS
            s0p = jnp.zeros((xp.shape[0], HG_HEADS, HG_KEY, HG_VAL), xp.dtype)
            yp, s_p = hgrn2_mixer(hp, s0p, hg_w_in[l], lb_all[layer], hg_norm[l], hg_w_o[l], min(HG_CHUNK, xp.shape[1]))
            ys, s_s = hgrn2_mixer(hs, state_hgrn[l], hg_w_in[l], lb_all[layer], hg_norm[l], hg_w_o[l], xs.shape[1])
            sp.append(s_p); ssl.append(s_s)
        xp = xp + yp
        xs = xs + ys
        xp = xp + sqrelu_mlp(rmsnorm(xp, norm_ffn[layer]), ffn_w1[layer], ffn_w2[layer])
        xs = xs + sqrelu_mlp(rmsnorm(xs, norm_ffn[layer]), ffn_w1[layer], ffn_w2[layer])
    y_prompt = rmsnorm(xp, norm_final)
    y_sample = rmsnorm(xs, norm_final)
    k_prompt = jnp.stack(kp)
    v_prompt = jnp.stack(vp)
    k_sample = jnp.stack(ksl)
    v_sample = jnp.stack(vsl)
    s_prompt = jnp.stack(sp)
    s_sample = jnp.stack(ssl)
    return (y_prompt, y_sample, k_prompt, v_prompt, k_sample, v_sample, s_prompt, s_sample)
```

```python
import functools
import math

import jax
import jax.numpy as jnp
from jax import lax
from jax.experimental import pallas as pl
from jax.experimental.pallas import tpu as pltpu

F32 = jnp.float32
BF16 = jnp.bfloat16
I32 = jnp.int32

N_HEADS = 8
HEAD_DIM = 128
MOBA_BLOCK = 256
MOBA_TOPK = 3
PAGE_SIZE = 128
N_BUCKETS = 32
MAX_DISTANCE = 128
MAX_EXACT = N_BUCKETS // 2
HG_HEADS = 8
HG_DIM = 128
EPS = 1e-6

LANES = 128
SUBLANES = 8
VMEM_LIMIT = 56 * 1024 * 1024

NEG = -1e30

NT_DIMS = (((1,), (1,)), ((), ()))
TN_DIMS = (((0,), (0,)), ((), ()))


def _params(sem):
    return pltpu.CompilerParams(dimension_semantics=sem, vmem_limit_bytes=VMEM_LIMIT)


def _rms(x, g):
    ms = jnp.mean(x * x, axis=-1, keepdims=True)
    return x * lax.rsqrt(ms + EPS) * g


def _t5_bias(dist, rb_get):
    n = jnp.maximum(dist, 0)
    nf = jnp.maximum(n, 1).astype(F32)
    large = MAX_EXACT + (jnp.log(nf / MAX_EXACT) / math.log(MAX_DISTANCE / MAX_EXACT)
                         * (N_BUCKETS - MAX_EXACT)).astype(I32)
    large = jnp.minimum(large, N_BUCKETS - 1)
    bucket = jnp.where(n < MAX_EXACT, n, large)
    out = jnp.zeros(dist.shape, F32)
    for k in range(N_BUCKETS):
        out = jnp.where(bucket == k, rb_get(k), out)
    return out


def _qkv_body(x_ref, g_ref, w_ref, q_ref, k_ref, v_ref):
    d = x_ref.shape[1]
    xn = _rms(x_ref[...], g_ref[...]).astype(BF16)
    for c, o_ref in enumerate((q_ref, k_ref, v_ref)):
        o_ref[...] = jnp.dot(xn, w_ref[:, c * d:(c + 1) * d], preferred_element_type=F32)


def _qkv_proj(x, g, w_bf16, tm):
    m, d = x.shape
    row = pl.BlockSpec((tm, d), lambda i: (i, 0))
    return pl.pallas_call(
        _qkv_body,
        out_shape=[jax.ShapeDtypeStruct((m, d), F32)] * 3,
        grid=(m // tm,),
        in_specs=[row, pl.BlockSpec((1, d), lambda i: (0, 0)),
                  pl.BlockSpec((d, 3 * d), lambda i: (0, 0))],
        out_specs=[row] * 3,
        compiler_params=_params(("arbitrary",)),
        name="qkv_proj",
    )(x, g.reshape(1, d), w_bf16)


def _hgrn_proj_body(layer, x_ref, g_ref, w_ref, lbp_ref, q_ref, k_ref, v_ref, lf_ref, gs_ref):
    d = x_ref.shape[1]
    xn = _rms(x_ref[...], g_ref[...]).astype(BF16)

    def proj(c):
        return jnp.dot(xn, w_ref[:, c * d:(c + 1) * d], preferred_element_type=F32)

    lbp = lbp_ref[...]
    e = jnp.exp(lbp - jnp.max(lbp, axis=0, keepdims=True))
    sm = e / jnp.sum(e, axis=0, keepdims=True)
    cum = sm[0:1]
    for r in range(1, layer + 1):
        cum = cum + sm[r:r + 1]
    lb = cum - sm[0:1]

    q_pre = proj(0)
    q_ref[...] = q_pre * jax.nn.sigmoid(q_pre) * (HG_DIM ** -0.5)
    fgate = lb + (1.0 - lb) * jax.nn.sigmoid(proj(1))
    lf_ref[...] = jnp.log(fgate)
    k_ref[...] = 1.0 - fgate
    v_ref[...] = proj(2)
    g_pre = proj(3)
    gs_ref[...] = g_pre * jax.nn.sigmoid(g_pre)


def _hgrn_proj(x, g, w_bf16, lb_param, layer, tm):
    m, d = x.shape
    depth = lb_param.shape[0]
    row = pl.BlockSpec((tm, d), lambda i: (i, 0))
    return pl.pallas_call(
        functools.partial(_hgrn_proj_body, layer),
        out_shape=[jax.ShapeDtypeStruct((m, d), F32)] * 5,
        grid=(m // tm,),
        in_specs=[row, pl.BlockSpec((1, d), lambda i: (0, 0)),
                  pl.BlockSpec((d, 4 * d), lambda i: (0, 0)),
                  pl.BlockSpec((depth, d), lambda i: (0, 0))],
        out_specs=[row] * 5,
        compiler_params=_params(("arbitrary",)),
        name="hgrn_proj",
    )(x, g.reshape(1, d), w_bf16, lb_param)


def _proj_mlp_body(final_norm, x_ref, o_ref, wo_ref, g_ref, w1_ref, w2_ref, gf_ref,
                   out_ref, x1_scr, xn_scr):
    f = pl.program_id(1)

    @pl.when(f == 0)
    def _():
        x1 = x_ref[...] + jnp.dot(o_ref[...], wo_ref[...], preferred_element_type=F32)
        x1_scr[...] = x1
        xn_scr[...] = _rms(x1, g_ref[...]).astype(BF16)
        out_ref[...] = jnp.zeros_like(out_ref)

    h = jnp.maximum(jnp.dot(xn_scr[...], w1_ref[...], preferred_element_type=F32), 0.0)
    out_ref[...] += jnp.dot((h * h).astype(BF16), w2_ref[...], preferred_element_type=F32)

    @pl.when(f == pl.num_programs(1) - 1)
    def _():
        y = x1_scr[...] + out_ref[...]
        if final_norm:
            y = _rms(y, gf_ref[...])
        out_ref[...] = y


def _proj_mlp(x, o_bf16, wo, g_ffn, w1, w2, g_final, final_norm, tm, tf):
    m, d = x.shape
    dff = w1.shape[1]
    row = pl.BlockSpec((tm, d), lambda i, f: (i, 0))
    vec = pl.BlockSpec((1, d), lambda i, f: (0, 0))
    return pl.pallas_call(
        functools.partial(_proj_mlp_body, final_norm),
        out_shape=jax.ShapeDtypeStruct((m, d), F32),
        grid=(m // tm, dff // tf),
        in_specs=[row, row, pl.BlockSpec((d, d), lambda i, f: (0, 0)), vec,
                  pl.BlockSpec((d, tf), lambda i, f: (0, f)),
                  pl.BlockSpec((tf, d), lambda i, f: (f, 0)), vec],
        out_specs=row,
        scratch_shapes=[pltpu.VMEM((tm, d), F32), pltpu.VMEM((tm, d), BF16)],
        compiler_params=_params(("arbitrary", "arbitrary")),
        name="proj_mlp",
    )(x, o_bf16, wo, g_ffn.reshape(1, d), w1, w2, g_final.reshape(1, d))


def _bias_tables_body(rb_ref, own_ref, adj_ref, far_ref):
    h = pl.program_id(0)
    shape = (MOBA_BLOCK, MOBA_BLOCK)
    d_own = lax.broadcasted_iota(I32, shape, 0) - lax.broadcasted_iota(I32, shape, 1)

    def rb_get(k):
        return rb_ref[h * N_BUCKETS + k]

    own_ref[0] = jnp.where(d_own >= 0, _t5_bias(d_own, rb_get), NEG)
    adj_ref[0] = _t5_bias(d_own + MOBA_BLOCK, rb_get)
    d_far = jnp.full((1, LANES), MOBA_BLOCK + 1, I32)
    far_ref[0] = _t5_bias(d_far, rb_get)


def _bias_tables(rb_flat):
    tile = pl.BlockSpec((1, MOBA_BLOCK, MOBA_BLOCK), lambda h: (h, 0, 0))
    return pl.pallas_call(
        _bias_tables_body,
        out_shape=[jax.ShapeDtypeStruct((N_HEADS, MOBA_BLOCK, MOBA_BLOCK), F32),
                   jax.ShapeDtypeStruct((N_HEADS, MOBA_BLOCK, MOBA_BLOCK), F32),
                   jax.ShapeDtypeStruct((N_HEADS, 1, LANES), F32)],
        grid=(N_HEADS,),
        in_specs=[pl.BlockSpec(memory_space=pltpu.SMEM)],
        out_specs=[tile, tile, pl.BlockSpec((1, 1, LANES), lambda h: (h, 0, 0))],
        compiler_params=_params(("arbitrary",)),
        name="bias_tables",
    )(rb_flat)


def _top3_mask(g, n_iota):
    width = float(g.shape[1])
    sel = jnp.zeros(g.shape, F32)
    for _ in range(MOBA_TOPK):
        mx = jnp.max(g, axis=1, keepdims=True)
        idx = jnp.min(jnp.where(g == mx, n_iota, width), axis=1, keepdims=True)
        hit = n_iota == idx
        sel = jnp.where(hit, 1.0, sel)
        g = jnp.where(hit, -jnp.inf, g)
    return sel


def _moba_prompt_body(q_ref, k_ref, v_ref, own_ref, adj_ref, far_ref, o_ref,
                      kb, vb, kmean, selb, m_s, l_s, acc_s):
    qi = pl.program_id(2)
    nb = kmean.shape[0]
    blk = MOBA_BLOCK

    @pl.when(qi == 0)
    def _():
        for n in range(nb):
            kblk = k_ref[0, n * blk:(n + 1) * blk, :]
            kb[n * blk:(n + 1) * blk, :] = kblk.astype(BF16)
            vb[n * blk:(n + 1) * blk, :] = v_ref[0, n * blk:(n + 1) * blk, :].astype(BF16)
            kmean[n:n + 1, :] = jnp.mean(kblk, axis=0, keepdims=True)

    q = q_ref[0]
    gate = lax.dot_general(q, kmean[...], NT_DIMS, precision=lax.Precision.HIGHEST,
                           preferred_element_type=F32)
    n_iota = lax.broadcasted_iota(I32, gate.shape, 1).astype(F32)
    valid = n_iota < qi.astype(F32)
    sel = _top3_mask(jnp.where(valid, gate, -jnp.inf), n_iota)
    mask = jnp.where(valid, jnp.where(sel > 0.0, 0.0, NEG), NEG)
    for n in range(nb):
        selb[n] = jnp.broadcast_to(mask[:, n:n + 1], (blk, LANES))

    qb = (q * (HEAD_DIM ** -0.5)).astype(BF16)
    m_s[...] = jnp.full(m_s.shape, NEG, F32)
    l_s[...] = jnp.zeros(l_s.shape, F32)
    acc_s[...] = jnp.zeros(acc_s.shape, F32)

    def attend(n, bias):
        start = pl.multiple_of(n * blk, blk)
        s = lax.dot_general(qb, kb[pl.ds(start, blk), :], NT_DIMS,
                            preferred_element_type=F32) + bias
        m_prev = m_s[...]
        m_new = jnp.maximum(m_prev, jnp.max(s, axis=1, keepdims=True))
        alpha = jnp.exp(m_prev - m_new)
        p = jnp.exp(s - jnp.concatenate([m_new, m_new], axis=1))
        l_s[...] = alpha * l_s[...] + jnp.sum(p, axis=1, keepdims=True)
        acc_s[...] = alpha * acc_s[...] + jnp.dot(
            p.astype(BF16), vb[pl.ds(start, blk), :], preferred_element_type=F32)
        m_s[...] = m_new

    attend(qi, own_ref[0])

    @pl.when(qi >= 1)
    def _():
        sm = selb[qi - 1]
        attend(qi - 1, adj_ref[0] + jnp.concatenate([sm, sm], axis=1))

    def far_block(n, carry):
        sm = selb[n] + far_ref[0]
        attend(n, jnp.concatenate([sm, sm], axis=1))
        return carry

    lax.fori_loop(0, jnp.maximum(qi - 1, 0), far_block, 0)
    o_ref[0] = (acc_s[...] / l_s[...]).astype(o_ref.dtype)


def _moba_prompt(q, k, v, own, adj, far):
    b, s, d = q.shape
    nb = s // MOBA_BLOCK
    blk = MOBA_BLOCK
    qspec = pl.BlockSpec((1, blk, HEAD_DIM), lambda bi, h, qi: (bi, qi, h))
    kvspec = pl.BlockSpec((1, s, HEAD_DIM), lambda bi, h, qi: (bi, 0, h))
    tile = pl.BlockSpec((1, blk, blk), lambda bi, h, qi: (h, 0, 0))
    return pl.pallas_call(
        _moba_prompt_body,
        out_shape=jax.ShapeDtypeStruct((b, s, d), BF16),
        grid=(b, N_HEADS, nb),
        in_specs=[qspec, kvspec, kvspec, tile, tile,
                  pl.BlockSpec((1, 1, LANES), lambda bi, h, qi: (h, 0, 0))],
        out_specs=qspec,
        scratch_shapes=[pltpu.VMEM((s, HEAD_DIM), BF16), pltpu.VMEM((s, HEAD_DIM), BF16),
                        pltpu.VMEM((nb, HEAD_DIM), F32), pltpu.VMEM((nb, blk, LANES), F32),
                        pltpu.VMEM((blk, LANES), F32), pltpu.VMEM((blk, LANES), F32),
                        pltpu.VMEM((blk, HEAD_DIM), F32)],
        compiler_params=_params(("arbitrary", "arbitrary", "arbitrary")),
        name="moba_prompt",
    )(q, k, v, own, adj, far)


PAGES_PER_STEP = 8


def _block_means_body(layer, pt_ref, ck_hbm, out_ref, buf, sem):
    bi = pl.program_id(0)
    ci = pl.program_id(1)
    nc = pl.num_programs(1)
    step = bi * nc + ci
    slot = step % 2

    def copies(b_idx, c_idx, sl):
        return [pltpu.make_async_copy(
            ck_hbm.at[layer, pt_ref[b_idx, c_idx * PAGES_PER_STEP + j]],
            buf.at[sl, j], sem.at[sl]) for j in range(PAGES_PER_STEP)]

    @pl.when(step == 0)
    def _():
        for cp in copies(0, 0, 0):
            cp.start()

    @pl.when(step + 1 < pl.num_programs(0) * nc)
    def _():
        nxt = step + 1
        for cp in copies(nxt // nc, nxt % nc, 1 - slot):
            cp.start()

    for cp in copies(bi, ci, slot):
        cp.wait()
    ppb = MOBA_BLOCK // PAGE_SIZE
    for blk in range(PAGES_PER_STEP // ppb):
        tot = jnp.sum(buf[slot, ppb * blk], axis=0)
        for j in range(1, ppb):
            tot = tot + jnp.sum(buf[slot, ppb * blk + j], axis=0)
        out_ref[0, blk] = tot * (1.0 / MOBA_BLOCK)


def _block_means(cache_k, page_table, layer):
    db, n_pages = page_table.shape
    nblk = n_pages * PAGE_SIZE // MOBA_BLOCK
    bps = PAGES_PER_STEP * PAGE_SIZE // MOBA_BLOCK
    return pl.pallas_call(
        functools.partial(_block_means_body, layer),
        out_shape=jax.ShapeDtypeStruct((db, nblk, N_HEADS, HEAD_DIM), F32),
        grid_spec=pltpu.PrefetchScalarGridSpec(
            num_scalar_prefetch=1,
            grid=(db, n_pages // PAGES_PER_STEP),
            in_specs=[pl.BlockSpec(memory_space=pl.ANY)],
            out_specs=pl.BlockSpec((1, bps, N_HEADS, HEAD_DIM), lambda b, c, pt: (b, c, 0, 0)),
            scratch_shapes=[pltpu.VMEM((2, PAGES_PER_STEP, PAGE_SIZE, N_HEADS, HEAD_DIM), F32),
                            pltpu.SemaphoreType.DMA((2,))]),
        compiler_params=_params(("arbitrary", "arbitrary")),
        name="block_means",
    )(page_table, cache_k)


def _sample_select_body(q_ref, bm_ref, sel_ref):
    nblk = bm_ref.shape[1] // N_HEADS
    q = q_ref[0]
    for h in range(N_HEADS):
        qh = q[:, h * HEAD_DIM:(h + 1) * HEAD_DIM]
        bmh = bm_ref[0, pl.ds(h, nblk, stride=N_HEADS), :]
        g = lax.dot_general(qh, bmh, NT_DIMS, precision=lax.Precision.HIGHEST,
                            preferred_element_type=F32)
        n_iota = lax.broadcasted_iota(I32, g.shape, 1).astype(F32)
        lane = lax.broadcasted_iota(I32, (g.shape[0], LANES), 1)
        out = jnp.zeros((g.shape[0], LANES), I32)
        for r in range(MOBA_TOPK):
            mx = jnp.max(g, axis=1, keepdims=True)
            idx = jnp.min(jnp.where(g == mx, n_iota, float(nblk)), axis=1, keepdims=True)
            out = jnp.where(lane == r, idx.astype(I32), out)
            g = jnp.where(n_iota == idx, -jnp.inf, g)
        sel_ref[0, h] = out


def _sample_select(q3, blk_mean):
    db, t, d = q3.shape
    nblk = blk_mean.shape[1]
    bm2 = blk_mean.reshape(db, nblk * N_HEADS, HEAD_DIM)
    return pl.pallas_call(
        _sample_select_body,
        out_shape=jax.ShapeDtypeStruct((db, N_HEADS, t, LANES), I32),
        grid=(db,),
        in_specs=[pl.BlockSpec((1, t, d), lambda b: (b, 0, 0)),
                  pl.BlockSpec((1, nblk * N_HEADS, HEAD_DIM), lambda b: (b, 0, 0))],
        out_specs=pl.BlockSpec((1, N_HEADS, t, LANES), lambda b: (b, 0, 0, 0)),
        compiler_params=_params(("arbitrary",)),
        name="sample_select",
    )(q3, bm2)


def _sample_attn_body(layer, past_len, pt_ref, sel_ref, q_ref, kn_ref, vn_ref, rb_ref,
                      ck_hbm, cv_hbm, o_ref, kbuf, vbuf, sem):
    bi = pl.program_id(0)
    h = pl.program_id(1)
    nh = pl.num_programs(1)
    t_len = q_ref.shape[1]
    ppb = MOBA_BLOCK // PAGE_SIZE
    n_sel = t_len * MOBA_TOPK
    step = bi * nh + h
    slot = step % 2

    def sel_at(b_idx, h_idx, j):
        return sel_ref[(b_idx * nh + h_idx) * n_sel + j]

    def copies(b_idx, h_idx, sl):
        out = []
        for j in range(n_sel):
            blk_id = sel_at(b_idx, h_idx, j)
            for pg in range(ppb):
                page = pt_ref[b_idx, blk_id * ppb + pg]
                dst = pl.ds(j * MOBA_BLOCK + pg * PAGE_SIZE, PAGE_SIZE)
                out.append(pltpu.make_async_copy(
                    ck_hbm.at[layer, page, :, h_idx, :], kbuf.at[sl, dst, :], sem.at[0, sl]))
                out.append(pltpu.make_async_copy(
                    cv_hbm.at[layer, page, :, h_idx, :], vbuf.at[sl, dst, :], sem.at[1, sl]))
        return out

    @pl.when(step == 0)
    def _():
        for cp in copies(0, 0, 0):
            cp.start()

    @pl.when(step + 1 < pl.num_programs(0) * nh)
    def _():
        nxt = step + 1
        for cp in copies(nxt // nh, nxt % nh, 1 - slot):
            cp.start()

    def rb_get(k):
        return rb_ref[h * N_BUCKETS + k]

    q = q_ref[0] * (HEAD_DIM ** -0.5)
    kn = kn_ref[0]
    vn = vn_ref[0]
    n_cols = n_sel * MOBA_BLOCK
    col = lax.broadcasted_iota(I32, (t_len, n_cols), 1)
    row = lax.broadcasted_iota(I32, (t_len, n_cols), 0)
    kpos = col & (MOBA_BLOCK - 1)
    for j in range(n_sel):
        in_seg = (col >= j * MOBA_BLOCK) & (col < (j + 1) * MOBA_BLOCK)
        kpos = kpos + jnp.where(in_seg, sel_at(bi, h, j) * MOBA_BLOCK, 0)
    per_q = MOBA_TOPK * MOBA_BLOCK
    mine = (col >= row * per_q) & (col < (row + 1) * per_q)
    bias = jnp.where(mine, _t5_bias(past_len + row - kpos, rb_get), NEG)

    row1 = lax.broadcasted_iota(I32, (t_len, LANES), 0)
    s_own = []
    for j in range(t_len):
        d_o = row1 - j
        s_j = jnp.sum(q * kn[j:j + 1, :], axis=1, keepdims=True)
        s_own.append(jnp.where(d_o >= 0, s_j + _t5_bias(d_o, rb_get), NEG)[:, 0:1])

    for cp in copies(bi, h, slot):
        cp.wait()
    s = lax.dot_general(q.astype(BF16), kbuf[slot].astype(BF16), NT_DIMS,
                        preferred_element_type=F32) + bias
    m = jnp.max(s, axis=1, keepdims=True)
    for s_j in s_own:
        m = jnp.maximum(m, s_j)
    p = jnp.exp(s - m)
    den = jnp.sum(p, axis=1, keepdims=True)
    acc = jnp.dot(p.astype(BF16), vbuf[slot].astype(BF16), preferred_element_type=F32)
    for j, s_j in enumerate(s_own):
        p_j = jnp.exp(s_j - m)
        den = den + p_j
        acc = acc + p_j * vn[j:j + 1, :]
    o_ref[0] = acc / den


def _sample_attn(q3, k3, v3, cache_k, cache_v, page_table, sel_flat, rb_flat, layer):
    db, t, d = q3.shape
    past_len = page_table.shape[1] * PAGE_SIZE
    n_cols = t * MOBA_TOPK * MOBA_BLOCK
    spec = pl.BlockSpec((1, t, HEAD_DIM), lambda b, h, pt, sl: (b, 0, h))
    anyspec = pl.BlockSpec(memory_space=pl.ANY)
    return pl.pallas_call(
        functools.partial(_sample_attn_body, layer, past_len),
        out_shape=jax.ShapeDtypeStruct((db, t, d), F32),
        grid_spec=pltpu.PrefetchScalarGridSpec(
            num_scalar_prefetch=2,
            grid=(db, N_HEADS),
            in_specs=[spec, spec, spec, pl.BlockSpec(memory_space=pltpu.SMEM), anyspec, anyspec],
            out_specs=spec,
            scratch_shapes=[pltpu.VMEM((2, n_cols, HEAD_DIM), F32),
                            pltpu.VMEM((2, n_cols, HEAD_DIM), F32),
                            pltpu.SemaphoreType.DMA((2, 2))]),
        compiler_params=_params(("arbitrary", "arbitrary")),
        name="sample_attn",
    )(page_table, sel_flat, q3, k3, v3, rb_flat, cache_k, cache_v)


SUB = 16


def _gla_body(q_ref, k_ref, v_ref, lf_ref, gs_ref, gn_ref, s0_ref, o_ref, sfin_ref, st_scr):
    ci = pl.program_id(1)
    chunk = q_ref.shape[0]
    nsub = chunk // SUB

    @pl.when(ci == 0)
    def _():
        for h in range(HG_HEADS):
            st_scr[h] = s0_ref[0, h].T

    r_i = lax.broadcasted_iota(I32, (chunk, chunk), 0)
    c_i = lax.broadcasted_iota(I32, (chunk, chunk), 1)
    tri = jnp.where(r_i >= c_i, 1.0, 0.0).astype(F32)
    b_all = jnp.dot(tri, lf_ref[...], precision=lax.Precision.HIGHEST,
                    preferred_element_type=F32)
    rows = lax.broadcasted_iota(I32, (chunk, HG_DIM), 0)
    sub_of_row = rows >> (SUB.bit_length() - 1)
    t_sub = lax.broadcasted_iota(I32, (SUB, HG_DIM), 0)

    for h in range(HG_HEADS):
        sl = slice(h * HG_DIM, (h + 1) * HG_DIM)
        b = b_all[:, sl]
        q = q_ref[:, sl]
        k = k_ref[:, sl]
        v = v_ref[:, sl]
        st = st_scr[h]
        b_last = b[chunk - 1:chunk, :]
        o = lax.dot_general((q * jnp.exp(b)).astype(BF16), st.astype(BF16), NT_DIMS,
                            preferred_element_type=F32)
        if nsub > 1:
            qs, ks = [], []
            for j in range(nsub - 1):
                e_j = b[SUB * j + SUB - 1:SUB * (j + 1), :]
                qs.append(jnp.where(sub_of_row > j, q * jnp.exp(jnp.minimum(b - e_j, 0.0)), 0.0))
                ks.append(jnp.where(sub_of_row == j, k * jnp.exp(jnp.minimum(e_j - b, 0.0)), 0.0))
            att = lax.dot_general(jnp.concatenate(qs, axis=1).astype(BF16),
                                  jnp.concatenate(ks, axis=1).astype(BF16), NT_DIMS,
                                  preferred_element_type=F32)
            o = o + jnp.dot(att.astype(BF16), v.astype(BF16), preferred_element_type=F32)
        diag = []
        for i in range(nsub):
            rs = slice(SUB * i, SUB * (i + 1))
            b_i, q_i, k_i, v_i = b[rs], q[rs], k[rs], v[rs]
            o_i = jnp.zeros((SUB, HG_DIM), F32)
            for s in range(SUB):
                dec = jnp.where(t_sub >= s, jnp.exp(jnp.minimum(b_i - b_i[s:s + 1, :], 0.0)), 0.0)
                a = jnp.sum(dec * q_i * k_i[s:s + 1, :], axis=1, keepdims=True)
                o_i = o_i + a * v_i[s:s + 1, :]
            diag.append(o_i)
        o = o + (jnp.concatenate(diag, axis=0) if nsub > 1 else diag[0])
        ke = k * jnp.exp(b_last - b)
        st_scr[h] = st * jnp.exp(b_last) + lax.dot_general(
            v.astype(BF16), ke.astype(BF16), TN_DIMS, preferred_element_type=F32)
        o = o * lax.rsqrt(jnp.mean(o * o, axis=-1, keepdims=True) + EPS)
        o_ref[:, sl] = (o * gn_ref[:, sl] * gs_ref[:, sl]).astype(o_ref.dtype)

    @pl.when(ci == pl.num_programs(1) - 1)
    def _():
        for h in range(HG_HEADS):
            sfin_ref[0, h] = st_scr[h].T


def _gla(q, k, v, lf, gs, g_norm, s0, chunk):
    m, d = q.shape
    nb = s0.shape[0]
    nc = m // nb // chunk
    row = pl.BlockSpec((chunk, d), lambda b, c: (b * nc + c, 0))
    state = pl.BlockSpec((1, HG_HEADS, HG_DIM, HG_DIM), lambda b, c: (b, 0, 0, 0))
    return pl.pallas_call(
        _gla_body,
        out_shape=[jax.ShapeDtypeStruct((m, d), BF16),
                   jax.ShapeDtypeStruct(s0.shape, F32)],
        grid=(nb, nc),
        in_specs=[row, row, row, row, row, pl.BlockSpec((1, d), lambda b, c: (0, 0)), state],
        out_specs=[row, state],
        scratch_shapes=[pltpu.VMEM((HG_HEADS, HG_DIM, HG_DIM), F32)],
        compiler_params=_params(("arbitrary", "arbitrary")),
        name="gla",
    )(q, k, v, lf, gs, g_norm.reshape(1, d), s0)


PROMPT_TM = 512
MLP_TM = 1024
MLP_TF = 512
GLA_CHUNK = 64
GLA_SAMPLE_CHUNK = 16


def kernel(x_prompt, x_sample, cache_k, cache_v, state_hgrn, page_table, norm_mix, norm_ffn,
           norm_final, attn_w_qkv, attn_w_o, rel_bias, hg_w_in, hg_lb_param, hg_norm, hg_w_o,
           ffn_w1, ffn_w2):
    b, s, d = x_prompt.shape
    db, t, _ = x_sample.shape
    mp, ms = b * s, db * t
    xp = x_prompt.reshape(mp, d)
    xs = x_sample.reshape(ms, d)
    rb_flat = rel_bias.T.reshape(-1)
    w_qkv = attn_w_qkv.astype(BF16)
    w_ao = attn_w_o.astype(BF16)
    w_in = hg_w_in.astype(BF16)
    w_ho = hg_w_o.astype(BF16)
    w1 = ffn_w1.astype(BF16)
    w2 = ffn_w2.astype(BF16)

    own, adj, far = _bias_tables(rb_flat)
    qp, kp, vp = _qkv_proj(xp, norm_mix[0], w_qkv[0], PROMPT_TM)
    op = _moba_prompt(qp.reshape(b, s, d), kp.reshape(b, s, d), vp.reshape(b, s, d),
                      own, adj, far)
    xp = _proj_mlp(xp, op.reshape(mp, d), w_ao[0], norm_ffn[0], w1[0], w2[0], norm_final,
                   False, MLP_TM, MLP_TF)

    qs, ks, vs = _qkv_proj(xs, norm_mix[0], w_qkv[0], ms)
    blk_mean = _block_means(cache_k, page_table, 0)
    q3 = qs.reshape(db, t, d)
    sel = _sample_select(q3, blk_mean)
    sel_flat = sel[:, :, :, :MOBA_TOPK].reshape(-1)
    os_ = _sample_attn(q3, ks.reshape(db, t, d), vs.reshape(db, t, d), cache_k, cache_v,
                       page_table, sel_flat, rb_flat, 0)
    xs = _proj_mlp(xs, os_.reshape(ms, d).astype(BF16), w_ao[0], norm_ffn[0], w1[0], w2[0],
                   norm_final, False, ms, MLP_TF)

    hq, hk, hv, hlf, hgs = _hgrn_proj(xp, norm_mix[1], w_in[0], hg_lb_param, 1, PROMPT_TM)
    s0p = jnp.zeros((b, HG_HEADS, HG_DIM, HG_DIM), F32)
    hop, s_prompt = _gla(hq, hk, hv, hlf, hgs, hg_norm[0], s0p, GLA_CHUNK)
    y_prompt = _proj_mlp(xp, hop, w_ho[0], norm_ffn[1], w1[1], w2[1], norm_final,
                         True, MLP_TM, MLP_TF)

    outs = _hgrn_proj(xs, norm_mix[1], w_in[0], hg_lb_param, 1, ms)
    pad = GLA_SAMPLE_CHUNK - t
    padded = [jnp.pad(a.reshape(db, t, d), ((0, 0), (0, pad), (0, 0))).reshape(
        db * GLA_SAMPLE_CHUNK, d) for a in outs]
    hos, s_sample = _gla(*padded, hg_norm[0], state_hgrn[0], GLA_SAMPLE_CHUNK)
    hos = hos.reshape(db, GLA_SAMPLE_CHUNK, d)[:, :t].reshape(ms, d)
    y_sample = _proj_mlp(xs, hos, w_ho[0], norm_ffn[1], w1[1], w2[1], norm_final,
                         True, ms, MLP_TF)

    hd = (N_HEADS, HEAD_DIM)
    return (y_prompt.reshape(b, s, d), y_sample.reshape(db, t, d),
            kp.reshape(1, b, s, *hd), vp.reshape(1, b, s, *hd),
            ks.reshape(1, db, t, *hd), vs.reshape(1, db, t, *hd),
            s_prompt[None], s_sample[None])
```

```python
import functools
import math

import jax
import jax.numpy as jnp
from jax import lax
from jax.experimental import pallas as pl
from jax.experimental.pallas import tpu as pltpu

F32 = jnp.float32
BF16 = jnp.bfloat16
I32 = jnp.int32

N_HEADS = 8
HEAD_DIM = 128
MOBA_BLOCK = 256
MOBA_TOPK = 3
PAGE_SIZE = 128
N_BUCKETS = 32
MAX_DISTANCE = 128
MAX_EXACT = N_BUCKETS // 2
HG_HEADS = 8
HG_DIM = 128
EPS = 1e-6

LANES = 128
SUBLANES = 8
VMEM_LIMIT = 56 * 1024 * 1024

NEG = -1e30
LOG2E = math.log2(math.e)

NT_DIMS = (((1,), (1,)), ((), ()))
TN_DIMS = (((0,), (0,)), ((), ()))


def _params(sem):
    return pltpu.CompilerParams(dimension_semantics=sem, vmem_limit_bytes=VMEM_LIMIT)


def _rms(x, g):
    ms = jnp.mean(x * x, axis=-1, keepdims=True)
    return x * lax.rsqrt(ms + EPS) * g


def _t5_bias(dist, rb_get):
    n = jnp.maximum(dist, 0)
    nf = jnp.maximum(n, 1).astype(F32)
    large = MAX_EXACT + (jnp.log(nf / MAX_EXACT) / math.log(MAX_DISTANCE / MAX_EXACT)
                         * (N_BUCKETS - MAX_EXACT)).astype(I32)
    large = jnp.minimum(large, N_BUCKETS - 1)
    bucket = jnp.where(n < MAX_EXACT, n, large)
    out = jnp.zeros(dist.shape, F32)
    for k in range(N_BUCKETS):
        out = jnp.where(bucket == k, rb_get(k), out)
    return out


def _qkv_body(x_ref, g_ref, w_ref, q_ref, k_ref, v_ref):
    d = x_ref.shape[1]
    xn = _rms(x_ref[...], g_ref[...]).astype(BF16)
    for c, o_ref in enumerate((q_ref, k_ref, v_ref)):
        o_ref[...] = jnp.dot(xn, w_ref[:, c * d:(c + 1) * d], preferred_element_type=F32)


def _qkv_proj(x, g, w_bf16, tm):
    m, d = x.shape
    row = pl.BlockSpec((tm, d), lambda i: (i, 0))
    return pl.pallas_call(
        _qkv_body,
        out_shape=[jax.ShapeDtypeStruct((m, d), F32)] * 3,
        grid=(m // tm,),
        in_specs=[row, pl.BlockSpec((1, d), lambda i: (0, 0)),
                  pl.BlockSpec((d, 3 * d), lambda i: (0, 0))],
        out_specs=[row] * 3,
        compiler_params=_params(("arbitrary",)),
        name="qkv_proj",
    )(x, g.reshape(1, d), w_bf16)


def _hgrn_proj_body(layer, x_ref, g_ref, w_ref, lbp_ref, q_ref, k_ref, v_ref, lf_ref, gs_ref):
    d = x_ref.shape[1]
    xn = _rms(x_ref[...], g_ref[...]).astype(BF16)

    def proj(c):
        return jnp.dot(xn, w_ref[:, c * d:(c + 1) * d], preferred_element_type=F32)

    lbp = lbp_ref[...]
    e = jnp.exp(lbp - jnp.max(lbp, axis=0, keepdims=True))
    sm = e / jnp.sum(e, axis=0, keepdims=True)
    cum = sm[0:1]
    for r in range(1, layer + 1):
        cum = cum + sm[r:r + 1]
    lb = cum - sm[0:1]

    q_pre = proj(0)
    q_ref[...] = q_pre * jax.nn.sigmoid(q_pre) * (HG_DIM ** -0.5)
    fgate = lb + (1.0 - lb) * jax.nn.sigmoid(proj(1))
    lf_ref[...] = jnp.log(fgate)
    k_ref[...] = 1.0 - fgate
    v_ref[...] = proj(2)
    g_pre = proj(3)
    gs_ref[...] = g_pre * jax.nn.sigmoid(g_pre)


def _hgrn_proj(x, g, w_bf16, lb_param, layer, tm):
    m, d = x.shape
    depth = lb_param.shape[0]
    row = pl.BlockSpec((tm, d), lambda i: (i, 0))
    return pl.pallas_call(
        functools.partial(_hgrn_proj_body, layer),
        out_shape=[jax.ShapeDtypeStruct((m, d), F32)] * 5,
        grid=(m // tm,),
        in_specs=[row, pl.BlockSpec((1, d), lambda i: (0, 0)),
                  pl.BlockSpec((d, 4 * d), lambda i: (0, 0)),
                  pl.BlockSpec((depth, d), lambda i: (0, 0))],
        out_specs=[row] * 5,
        compiler_params=_params(("arbitrary",)),
        name="hgrn_proj",
    )(x, g.reshape(1, d), w_bf16, lb_param)


def _proj_mlp_body(final_norm, x_ref, o_ref, wo_ref, g_ref, w1_ref, w2_ref, gf_ref,
                   out_ref, x1_scr, xn_scr):
    f = pl.program_id(1)

    @pl.when(f == 0)
    def _():
        x1 = x_ref[...] + jnp.dot(o_ref[...], wo_ref[...], preferred_element_type=F32)
        x1_scr[...] = x1
        xn_scr[...] = _rms(x1, g_ref[...]).astype(BF16)
        out_ref[...] = jnp.zeros_like(out_ref)

    h = jnp.maximum(jnp.dot(xn_scr[...], w1_ref[...], preferred_element_type=F32), 0.0)
    out_ref[...] += jnp.dot((h * h).astype(BF16), w2_ref[...], preferred_element_type=F32)

    @pl.when(f == pl.num_programs(1) - 1)
    def _():
        y = x1_scr[...] + out_ref[...]
        if final_norm:
            y = _rms(y, gf_ref[...])
        out_ref[...] = y


def _proj_mlp(x, o_bf16, wo, g_ffn, w1, w2, g_final, final_norm, tm, tf):
    m, d = x.shape
    dff = w1.shape[1]
    row = pl.BlockSpec((tm, d), lambda i, f: (i, 0))
    vec = pl.BlockSpec((1, d), lambda i, f: (0, 0))
    return pl.pallas_call(
        functools.partial(_proj_mlp_body, final_norm),
        out_shape=jax.ShapeDtypeStruct((m, d), F32),
        grid=(m // tm, dff // tf),
        in_specs=[row, row, pl.BlockSpec((d, d), lambda i, f: (0, 0)), vec,
                  pl.BlockSpec((d, tf), lambda i, f: (0, f)),
                  pl.BlockSpec((tf, d), lambda i, f: (f, 0)), vec],
        out_specs=row,
        scratch_shapes=[pltpu.VMEM((tm, d), F32), pltpu.VMEM((tm, d), BF16)],
        compiler_params=_params(("arbitrary", "arbitrary")),
        name="proj_mlp",
    )(x, o_bf16, wo, g_ffn.reshape(1, d), w1, w2, g_final.reshape(1, d))


def _bias_tables_body(rb_ref, own_ref, adj_ref, far_ref, last_ref):
    h = pl.program_id(0)
    shape = (MOBA_BLOCK, MOBA_BLOCK)
    d_own = lax.broadcasted_iota(I32, shape, 1) - lax.broadcasted_iota(I32, shape, 0)

    def rb_get(k):
        return rb_ref[h * N_BUCKETS + k] * LOG2E

    own_ref[0] = jnp.where(d_own >= 0, _t5_bias(d_own, rb_get), NEG)
    adj_ref[0] = _t5_bias(d_own + MOBA_BLOCK, rb_get)
    d_far = jnp.full((1, LANES), MOBA_BLOCK + 1, I32)
    far_ref[0] = _t5_bias(d_far, rb_get)
    lshape = (SUBLANES, MOBA_BLOCK)
    d_last = (MOBA_BLOCK + lax.broadcasted_iota(I32, lshape, 0)
              - lax.broadcasted_iota(I32, lshape, 1))
    last_ref[0] = _t5_bias(d_last, rb_get)


def _bias_tables(rb_flat):
    tile = pl.BlockSpec((1, MOBA_BLOCK, MOBA_BLOCK), lambda h: (h, 0, 0))
    return pl.pallas_call(
        _bias_tables_body,
        out_shape=[jax.ShapeDtypeStruct((N_HEADS, MOBA_BLOCK, MOBA_BLOCK), F32),
                   jax.ShapeDtypeStruct((N_HEADS, MOBA_BLOCK, MOBA_BLOCK), F32),
                   jax.ShapeDtypeStruct((N_HEADS, 1, LANES), F32),
                   jax.ShapeDtypeStruct((N_HEADS, SUBLANES, MOBA_BLOCK), F32)],
        grid=(N_HEADS,),
        in_specs=[pl.BlockSpec(memory_space=pltpu.SMEM)],
        out_specs=[tile, tile, pl.BlockSpec((1, 1, LANES), lambda h: (h, 0, 0)),
                   pl.BlockSpec((1, SUBLANES, MOBA_BLOCK), lambda h: (h, 0, 0))],
        compiler_params=_params(("arbitrary",)),
        name="bias_tables",
    )(rb_flat)


PAD_BLOCKS = 2
NB_MAX = 32
AUG = 2 * HEAD_DIM
FLAG_COL = 2 * NB_MAX
HEADS_PER_STEP = 4


def _moba_prompt_body(q_ref, k_ref, v_ref, own_ref, adj_ref, far_ref, o_ref,
                      kaug, vt, kmean, qaug, s_scr, p_scr, acc_s):
    qi = pl.program_id(2)
    nb = kmean.shape[1]
    blk = MOBA_BLOCK
    heads = range(HEADS_PER_STEP)

    def hcols(i):
        return slice(i * HEAD_DIM, (i + 1) * HEAD_DIM)

    @pl.when(qi == 0)
    def _():
        lane_p = lax.broadcasted_iota(I32, (PAD_BLOCKS * blk, AUG), 1)
        pad_keys = jnp.where(lane_p == HEAD_DIM + FLAG_COL, 1.0, 0.0).astype(BF16)
        row_q = lax.broadcasted_iota(I32, (AUG - HEAD_DIM, blk), 0)
        for i in heads:
            kaug[i, 0:PAD_BLOCKS * blk, :] = pad_keys
            for n in range(PAD_BLOCKS):
                vt[i, n] = jnp.zeros((HEAD_DIM, blk), BF16)
            kmean[i] = jnp.zeros((nb, HEAD_DIM), F32)
            qaug[i, HEAD_DIM:AUG, :] = jnp.where(row_q == FLAG_COL, NEG, 0.0).astype(BF16)

    lane = lax.broadcasted_iota(I32, (blk, LANES), 1)
    hot = jnp.where(lane == qi, 1.0, jnp.where(lane == NB_MAX + qi, 1.0, 0.0)).astype(BF16)
    row_n = lax.broadcasted_iota(I32, (nb, HEAD_DIM), 0)
    own_rows = pl.ds(pl.multiple_of((qi + PAD_BLOCKS) * blk, blk), blk)
    for i in heads:
        kblk = k_ref[0, :, hcols(i)]
        kaug[i, own_rows, 0:HEAD_DIM] = kblk.astype(BF16)
        kaug[i, own_rows, HEAD_DIM:AUG] = hot
        vt[i, qi + PAD_BLOCKS] = v_ref[0, :, hcols(i)].T.astype(BF16)
        kmean[i] = jnp.where(row_n == qi, jnp.mean(kblk, axis=0, keepdims=True), kmean[i])

    qf = qi.astype(F32)

    def write_query_side(i):
        q = q_ref[0, :, hcols(i)]
        gate = lax.dot_general(kmean[i], q, NT_DIMS, precision=lax.Precision.HIGHEST,
                               preferred_element_type=F32)
        n_io = lax.broadcasted_iota(I32, gate.shape, 0).astype(F32)
        g = jnp.where(n_io < qf, gate, -jnp.inf)
        sel = jnp.zeros(gate.shape, F32)
        for _ in range(MOBA_TOPK):
            mx = jnp.max(g, axis=0, keepdims=True)
            idx = jnp.min(jnp.where(g == mx, n_io, float(nb)), axis=0, keepdims=True)
            hit = n_io == idx
            sel = jnp.where(hit, 1.0, sel)
            g = jnp.where(hit, -jnp.inf, g)
        far = jnp.concatenate([far_ref[i]] * (blk // LANES), axis=1)
        picked = jnp.where(sel > 0.0, jnp.where(n_io < qf - 1.0, far, 0.0), NEG)
        bias = jnp.where(n_io < qf, picked, jnp.where(n_io == qf, 0.0, NEG))
        bias_hi = bias.astype(BF16)
        qaug[i, 0:HEAD_DIM, :] = (q * (HEAD_DIM ** -0.5 * LOG2E)).T.astype(BF16)
        qaug[i, HEAD_DIM:HEAD_DIM + nb, :] = bias_hi
        qaug[i, HEAD_DIM + NB_MAX:HEAD_DIM + NB_MAX + nb, :] = (
            bias - bias_hi.astype(F32)).astype(BF16)

    def scores(i, padded_block):
        r0 = pl.multiple_of(padded_block * blk, blk)
        return jnp.dot(kaug[i, pl.ds(r0, blk), :], qaug[i], preferred_element_type=F32)

    for i in heads:
        write_query_side(i)
        s_scr[i] = scores(i, qi + PAD_BLOCKS) + own_ref[i]
        acc_s[i] = jnp.zeros((HEAD_DIM, blk), F32)

    carry0 = []
    for i in heads:
        s_next = scores(i, qi - 1 + PAD_BLOCKS) + adj_ref[i]
        s_cur = s_scr[i]
        m0 = jnp.max(s_cur, axis=0, keepdims=True)
        p = jnp.exp2(s_cur - m0)
        p_scr[i] = p.astype(BF16)
        s_scr[i] = s_next
        carry0 += [m0, jnp.sum(p, axis=0, keepdims=True), jnp.ones_like(m0)]

    def step(t, carry):
        out = []
        for i in heads:
            m_prev, l_prev, alpha_prev = carry[3 * i:3 * i + 3]
            s_next = scores(i, qi - t - 1 + PAD_BLOCKS)
            pv = jnp.dot(vt[i, qi - t + 1 + PAD_BLOCKS], p_scr[i], preferred_element_type=F32)
            acc_s[i] = alpha_prev * acc_s[i] + pv
            s_cur = s_scr[i]
            m_new = jnp.maximum(m_prev, jnp.max(s_cur, axis=0, keepdims=True))
            alpha = jnp.exp2(m_prev - m_new)
            p = jnp.exp2(s_cur - m_new)
            p_scr[i] = p.astype(BF16)
            s_scr[i] = s_next
            out += [m_new, alpha * l_prev + jnp.sum(p, axis=0, keepdims=True), alpha]
        return tuple(out)

    fin = lax.fori_loop(1, qi + 2, step, tuple(carry0))
    for i in heads:
        o_ref[0, :, hcols(i)] = (acc_s[i] / fin[3 * i + 1]).T.astype(o_ref.dtype)


def _moba_prompt(q, k, v, own, adj, far):
    b, s, d = q.shape
    nb = s // MOBA_BLOCK
    assert nb <= NB_MAX and FLAG_COL < AUG - HEAD_DIM and N_HEADS % HEADS_PER_STEP == 0
    blk = MOBA_BLOCK
    hp = HEADS_PER_STEP
    qspec = pl.BlockSpec((1, blk, hp * HEAD_DIM), lambda bi, h, qi: (bi, qi, h))
    kvspec = qspec
    tile = pl.BlockSpec((hp, blk, blk), lambda bi, h, qi: (h, 0, 0))
    return pl.pallas_call(
        _moba_prompt_body,
        out_shape=jax.ShapeDtypeStruct((b, s, d), BF16),
        grid=(b, N_HEADS // hp, nb),
        in_specs=[qspec, kvspec, kvspec, tile, tile,
                  pl.BlockSpec((hp, 1, LANES), lambda bi, h, qi: (h, 0, 0))],
        out_specs=qspec,
        scratch_shapes=[pltpu.VMEM((hp, (nb + PAD_BLOCKS) * blk, AUG), BF16),
                        pltpu.VMEM((hp, nb + PAD_BLOCKS, HEAD_DIM, blk), BF16),
                        pltpu.VMEM((hp, nb, HEAD_DIM), F32),
                        pltpu.VMEM((hp, AUG, blk), BF16),
                        pltpu.VMEM((hp, blk, blk), F32),
                        pltpu.VMEM((hp, blk, blk), BF16),
                        pltpu.VMEM((hp, HEAD_DIM, blk), F32)],
        compiler_params=_params(("arbitrary", "arbitrary", "arbitrary")),
        name="moba_prompt",
    )(q, k, v, own, adj, far)


PAGES_PER_STEP = 8


def _block_means_body(layer, pt_ref, ck_hbm, out_ref, buf, sem):
    bi = pl.program_id(0)
    ci = pl.program_id(1)
    nc = pl.num_programs(1)
    step = bi * nc + ci
    slot = step % 2

    def copies(b_idx, c_idx, sl):
        return [pltpu.make_async_copy(
            ck_hbm.at[layer, pt_ref[b_idx, c_idx * PAGES_PER_STEP + j]],
            buf.at[sl, j], sem.at[sl]) for j in range(PAGES_PER_STEP)]

    @pl.when(step == 0)
    def _():
        for cp in copies(0, 0, 0):
            cp.start()

    @pl.when(step + 1 < pl.num_programs(0) * nc)
    def _():
        nxt = step + 1
        for cp in copies(nxt // nc, nxt % nc, 1 - slot):
            cp.start()

    for cp in copies(bi, ci, slot):
        cp.wait()
    ppb = MOBA_BLOCK // PAGE_SIZE
    for blk in range(PAGES_PER_STEP // ppb):
        tot = jnp.sum(buf[slot, ppb * blk], axis=0)
        for j in range(1, ppb):
            tot = tot + jnp.sum(buf[slot, ppb * blk + j], axis=0)
        out_ref[0, blk] = tot * (1.0 / MOBA_BLOCK)


def _block_means(cache_k, page_table, layer):
    db, n_pages = page_table.shape
    nblk = n_pages * PAGE_SIZE // MOBA_BLOCK
    bps = PAGES_PER_STEP * PAGE_SIZE // MOBA_BLOCK
    return pl.pallas_call(
        functools.partial(_block_means_body, layer),
        out_shape=jax.ShapeDtypeStruct((db, nblk, N_HEADS, HEAD_DIM), F32),
        grid_spec=pltpu.PrefetchScalarGridSpec(
            num_scalar_prefetch=1,
            grid=(db, n_pages // PAGES_PER_STEP),
            in_specs=[pl.BlockSpec(memory_space=pl.ANY)],
            out_specs=pl.BlockSpec((1, bps, N_HEADS, HEAD_DIM), lambda b, c, pt: (b, c, 0, 0)),
            scratch_shapes=[pltpu.VMEM((2, PAGES_PER_STEP, PAGE_SIZE, N_HEADS, HEAD_DIM), F32),
                            pltpu.SemaphoreType.DMA((2,))]),
        compiler_params=_params(("arbitrary", "arbitrary")),
        name="block_means",
    )(page_table, cache_k)


def _sample_select_body(q_ref, bm_ref, sel_ref):
    nblk = bm_ref.shape[1] // N_HEADS
    q = q_ref[0]
    for h in range(N_HEADS):
        qh = q[:, h * HEAD_DIM:(h + 1) * HEAD_DIM]
        bmh = bm_ref[0, pl.ds(h, nblk, stride=N_HEADS), :]
        g = lax.dot_general(qh, bmh, NT_DIMS, precision=lax.Precision.HIGHEST,
                            preferred_element_type=F32)
        n_iota = lax.broadcasted_iota(I32, g.shape, 1).astype(F32)
        lane = lax.broadcasted_iota(I32, (g.shape[0], LANES), 1)
        out = jnp.zeros((g.shape[0], LANES), I32)
        for r in range(MOBA_TOPK):
            mx = jnp.max(g, axis=1, keepdims=True)
            idx = jnp.min(jnp.where(g == mx, n_iota, float(nblk)), axis=1, keepdims=True)
            out = jnp.where(lane == r, idx.astype(I32), out)
            g = jnp.where(n_iota == idx, -jnp.inf, g)
        sel_ref[0, h] = out


def _sample_select(q3, blk_mean):
    db, t, d = q3.shape
    nblk = blk_mean.shape[1]
    bm2 = blk_mean.reshape(db, nblk * N_HEADS, HEAD_DIM)
    return pl.pallas_call(
        _sample_select_body,
        out_shape=jax.ShapeDtypeStruct((db, N_HEADS, t, LANES), I32),
        grid=(db,),
        in_specs=[pl.BlockSpec((1, t, d), lambda b: (b, 0, 0)),
                  pl.BlockSpec((1, nblk * N_HEADS, HEAD_DIM), lambda b: (b, 0, 0))],
        out_specs=pl.BlockSpec((1, N_HEADS, t, LANES), lambda b: (b, 0, 0, 0)),
        compiler_params=_params(("arbitrary",)),
        name="sample_select",
    )(q3, bm2)


def _sample_attn_body(layer, past_len, pt_ref, sel_ref, q_ref, kn_ref, vn_ref, far_ref, last_ref,
                      rb_ref, ck_hbm, cv_hbm, o_ref, kbuf, vbuf, sem):
    bi = pl.program_id(0)
    h = pl.program_id(1)
    nh = pl.num_programs(1)
    t_len = q_ref.shape[1]
    ppb = MOBA_BLOCK // PAGE_SIZE
    n_sel = t_len * MOBA_TOPK
    step = bi * nh + h
    slot = step % 2

    def sel_at(b_idx, h_idx, j):
        return sel_ref[(b_idx * nh + h_idx) * n_sel + j]

    def copies(b_idx, h_idx, sl):
        out = []
        for j in range(n_sel):
            blk_id = sel_at(b_idx, h_idx, j)
            for pg in range(ppb):
                page = pt_ref[b_idx, blk_id * ppb + pg]
                dst = pl.ds(j * MOBA_BLOCK + pg * PAGE_SIZE, PAGE_SIZE)
                out.append(pltpu.make_async_copy(
                    ck_hbm.at[layer, page, :, h_idx, :], kbuf.at[sl, dst, :], sem.at[0, sl]))
                out.append(pltpu.make_async_copy(
                    cv_hbm.at[layer, page, :, h_idx, :], vbuf.at[sl, dst, :], sem.at[1, sl]))
        return out

    @pl.when(step == 0)
    def _():
        for cp in copies(0, 0, 0):
            cp.start()

    @pl.when(step + 1 < pl.num_programs(0) * nh)
    def _():
        nxt = step + 1
        for cp in copies(nxt // nh, nxt % nh, 1 - slot):
            cp.start()

    def rb_get(k):
        return rb_ref[h * N_BUCKETS + k] * LOG2E

    q = q_ref[0] * (HEAD_DIM ** -0.5 * LOG2E)
    kn = kn_ref[0]
    vn = vn_ref[0]
    n_cols = n_sel * MOBA_BLOCK
    col = lax.broadcasted_iota(I32, (t_len, n_cols), 1)
    row = lax.broadcasted_iota(I32, (t_len, n_cols), 0)
    far = jnp.broadcast_to(jnp.concatenate([far_ref[0]] * (MOBA_BLOCK // LANES), axis=1),
                           (t_len, MOBA_BLOCK))
    last = last_ref[0, 0:t_len, :]
    last_block = past_len // MOBA_BLOCK - 1
    bias = jnp.concatenate(
        [jnp.where(sel_at(bi, h, j) == last_block, last, far) for j in range(n_sel)], axis=1)
    per_q = MOBA_TOPK * MOBA_BLOCK
    mine = (col >= row * per_q) & (col < (row + 1) * per_q)
    bias = jnp.where(mine, bias, NEG)

    row1 = lax.broadcasted_iota(I32, (t_len, LANES), 0)
    s_own = []
    for j in range(t_len):
        d_o = row1 - j
        s_j = jnp.sum(q * kn[j:j + 1, :], axis=1, keepdims=True)
        s_own.append(jnp.where(d_o >= 0, s_j + _t5_bias(d_o, rb_get), NEG)[:, 0:1])

    for cp in copies(bi, h, slot):
        cp.wait()
    s = lax.dot_general(q.astype(BF16), kbuf[slot].astype(BF16), NT_DIMS,
                        preferred_element_type=F32) + bias
    m = jnp.max(s, axis=1, keepdims=True)
    for s_j in s_own:
        m = jnp.maximum(m, s_j)
    p = jnp.exp2(s - m)
    den = jnp.sum(p, axis=1, keepdims=True)
    acc = jnp.dot(p.astype(BF16), vbuf[slot].astype(BF16), preferred_element_type=F32)
    for j, s_j in enumerate(s_own):
        p_j = jnp.exp2(s_j - m)
        den = den + p_j
        acc = acc + p_j * vn[j:j + 1, :]
    o_ref[0] = acc / den


def _sample_attn(q3, k3, v3, far, last, cache_k, cache_v, page_table, sel_flat, rb_flat, layer):
    db, t, d = q3.shape
    past_len = page_table.shape[1] * PAGE_SIZE
    assert past_len % MOBA_BLOCK == 0 and t <= SUBLANES
    n_cols = t * MOBA_TOPK * MOBA_BLOCK
    spec = pl.BlockSpec((1, t, HEAD_DIM), lambda b, h, pt, sl: (b, 0, h))
    farspec = pl.BlockSpec((1, 1, LANES), lambda b, h, pt, sl: (h, 0, 0))
    lastspec = pl.BlockSpec((1, SUBLANES, MOBA_BLOCK), lambda b, h, pt, sl: (h, 0, 0))
    anyspec = pl.BlockSpec(memory_space=pl.ANY)
    return pl.pallas_call(
        functools.partial(_sample_attn_body, layer, past_len),
        out_shape=jax.ShapeDtypeStruct((db, t, d), F32),
        grid_spec=pltpu.PrefetchScalarGridSpec(
            num_scalar_prefetch=2,
            grid=(db, N_HEADS),
            in_specs=[spec, spec, spec, farspec, lastspec,
                      pl.BlockSpec(memory_space=pltpu.SMEM), anyspec, anyspec],
            out_specs=spec,
            scratch_shapes=[pltpu.VMEM((2, n_cols, HEAD_DIM), F32),
                            pltpu.VMEM((2, n_cols, HEAD_DIM), F32),
                            pltpu.SemaphoreType.DMA((2, 2))]),
        compiler_params=_params(("arbitrary", "arbitrary")),
        name="sample_attn",
    )(page_table, sel_flat, q3, k3, v3, far, last, rb_flat, cache_k, cache_v)


SUB = 16


def _gla_body(q_ref, k_ref, v_ref, lf_ref, gs_ref, gn_ref, s0_ref, o_ref, sfin_ref, st_scr):
    ci = pl.program_id(1)
    chunk = q_ref.shape[0]
    nsub = chunk // SUB

    @pl.when(ci == 0)
    def _():
        for h in range(HG_HEADS):
            st_scr[h] = s0_ref[0, h].T

    r_i = lax.broadcasted_iota(I32, (chunk, chunk), 0)
    c_i = lax.broadcasted_iota(I32, (chunk, chunk), 1)
    tri = jnp.where(r_i >= c_i, 1.0, 0.0).astype(F32)
    b_all = jnp.dot(tri, lf_ref[...], precision=lax.Precision.HIGHEST,
                    preferred_element_type=F32) * LOG2E
    neg_inf = -jnp.inf
    rows = lax.broadcasted_iota(I32, (chunk, HG_DIM), 0)
    sub_of_row = rows >> (SUB.bit_length() - 1)
    t_sub = lax.broadcasted_iota(I32, (SUB, HG_DIM), 0)

    for h in range(HG_HEADS):
        sl = slice(h * HG_DIM, (h + 1) * HG_DIM)
        b = b_all[:, sl]
        q = q_ref[:, sl]
        k = k_ref[:, sl]
        v = v_ref[:, sl]
        st = st_scr[h]
        b_last = b[chunk - 1:chunk, :]
        o = lax.dot_general((q * jnp.exp2(b)).astype(BF16), st.astype(BF16), NT_DIMS,
                            preferred_element_type=F32)
        if nsub > 1:
            qs, ks = [], []
            for j in range(nsub - 1):
                e_j = b[SUB * j + SUB - 1:SUB * (j + 1), :]
                qs.append(q * jnp.exp2(jnp.where(sub_of_row > j, b - e_j, neg_inf)))
                ks.append(k * jnp.exp2(jnp.where(sub_of_row == j, e_j - b, neg_inf)))
            att = lax.dot_general(jnp.concatenate(qs, axis=1).astype(BF16),
                                  jnp.concatenate(ks, axis=1).astype(BF16), NT_DIMS,
                                  preferred_element_type=F32)
            o = o + jnp.dot(att.astype(BF16), v.astype(BF16), preferred_element_type=F32)
        diag = []
        for i in range(nsub):
            rs = slice(SUB * i, SUB * (i + 1))
            b_i, q_i, k_i, v_i = b[rs], q[rs], k[rs], v[rs]
            o_i = jnp.zeros((SUB, HG_DIM), F32)
            for s in range(SUB):
                dec = jnp.exp2(jnp.where(t_sub >= s, b_i - b_i[s:s + 1, :], neg_inf))
                a = jnp.sum(dec * q_i * k_i[s:s + 1, :], axis=1, keepdims=True)
                o_i = o_i + a * v_i[s:s + 1, :]
            diag.append(o_i)
        o = o + (jnp.concatenate(diag, axis=0) if nsub > 1 else diag[0])
        ke = k * jnp.exp2(b_last - b)
        st_scr[h] = st * jnp.exp2(b_last) + lax.dot_general(
            v.astype(BF16), ke.astype(BF16), TN_DIMS, preferred_element_type=F32)
        o = o * lax.rsqrt(jnp.mean(o * o, axis=-1, keepdims=True) + EPS)
        o_ref[:, sl] = (o * gn_ref[:, sl] * gs_ref[:, sl]).astype(o_ref.dtype)

    @pl.when(ci == pl.num_programs(1) - 1)
    def _():
        for h in range(HG_HEADS):
            sfin_ref[0, h] = st_scr[h].T


def _gla(q, k, v, lf, gs, g_norm, s0, chunk):
    m, d = q.shape
    nb = s0.shape[0]
    nc = m // nb // chunk
    row = pl.BlockSpec((chunk, d), lambda b, c: (b * nc + c, 0))
    state = pl.BlockSpec((1, HG_HEADS, HG_DIM, HG_DIM), lambda b, c: (b, 0, 0, 0))
    return pl.pallas_call(
        _gla_body,
        out_shape=[jax.ShapeDtypeStruct((m, d), BF16),
                   jax.ShapeDtypeStruct(s0.shape, F32)],
        grid=(nb, nc),
        in_specs=[row, row, row, row, row, pl.BlockSpec((1, d), lambda b, c: (0, 0)), state],
        out_specs=[row, state],
        scratch_shapes=[pltpu.VMEM((HG_HEADS, HG_DIM, HG_DIM), F32)],
        compiler_params=_params(("arbitrary", "arbitrary")),
        name="gla",
    )(q, k, v, lf, gs, g_norm.reshape(1, d), s0)


PROMPT_TM = 512
MLP_TM = 1024
MLP_TF = 512
GLA_CHUNK = 64
GLA_SAMPLE_CHUNK = 16


def kernel(x_prompt, x_sample, cache_k, cache_v, state_hgrn, page_table, norm_mix, norm_ffn,
           norm_final, attn_w_qkv, attn_w_o, rel_bias, hg_w_in, hg_lb_param, hg_norm, hg_w_o,
           ffn_w1, ffn_w2):
    b, s, d = x_prompt.shape
    db, t, _ = x_sample.shape
    mp, ms = b * s, db * t
    xp = x_prompt.reshape(mp, d)
    xs = x_sample.reshape(ms, d)
    rb_flat = rel_bias.T.reshape(-1)
    w_qkv = attn_w_qkv.astype(BF16)
    w_ao = attn_w_o.astype(BF16)
    w_in = hg_w_in.astype(BF16)
    w_ho = hg_w_o.astype(BF16)
    w1 = ffn_w1.astype(BF16)
    w2 = ffn_w2.astype(BF16)

    own, adj, far, last = _bias_tables(rb_flat)
    qp, kp, vp = _qkv_proj(xp, norm_mix[0], w_qkv[0], PROMPT_TM)
    op = _moba_prompt(qp.reshape(b, s, d), kp.reshape(b, s, d), vp.reshape(b, s, d),
                      own, adj, far)
    xp = _proj_mlp(xp, op.reshape(mp, d), w_ao[0], norm_ffn[0], w1[0], w2[0], norm_final,
                   False, MLP_TM, MLP_TF)

    qs, ks, vs = _qkv_proj(xs, norm_mix[0], w_qkv[0], ms)
    blk_mean = _block_means(cache_k, page_table, 0)
    q3 = qs.reshape(db, t, d)
    sel = _sample_select(q3, blk_mean)
    sel_flat = sel[:, :, :, :MOBA_TOPK].reshape(-1)
    os_ = _sample_attn(q3, ks.reshape(db, t, d), vs.reshape(db, t, d), far, last,
                       cache_k, cache_v, page_table, sel_flat, rb_flat, 0)
    xs = _proj_mlp(xs, os_.reshape(ms, d).astype(BF16), w_ao[0], norm_ffn[0], w1[0], w2[0],
                   norm_final, False, ms, MLP_TF)

    hq, hk, hv, hlf, hgs = _hgrn_proj(xp, norm_mix[1], w_in[0], hg_lb_param, 1, PROMPT_TM)
    s0p = jnp.zeros((b, HG_HEADS, HG_DIM, HG_DIM), F32)
    hop, s_prompt = _gla(hq, hk, hv, hlf, hgs, hg_norm[0], s0p, GLA_CHUNK)
    y_prompt = _proj_mlp(xp, hop, w_ho[0], norm_ffn[1], w1[1], w2[1], norm_final,
                         True, MLP_TM, MLP_TF)

    outs = _hgrn_proj(xs, norm_mix[1], w_in[0], hg_lb_param, 1, ms)
    pad = GLA_SAMPLE_CHUNK - t
    padded = [jnp.pad(a.reshape(db, t, d), ((0, 0), (0, pad), (0, 0))).reshape(
        db * GLA_SAMPLE_CHUNK, d) for a in outs]
    hos, s_sample = _gla(*padded, hg_norm[0], state_hgrn[0], GLA_SAMPLE_CHUNK)
    hos = hos.reshape(db, GLA_SAMPLE_CHUNK, d)[:, :t].reshape(ms, d)
    y_sample = _proj_mlp(xs, hos, w_ho[0], norm_ffn[1], w1[1], w2[1], norm_final,
                         True, ms, MLP_TF)

    hd = (N_HEADS, HEAD_DIM)
    return (y_prompt.reshape(b, s, d), y_sample.reshape(db, t, d),
            kp.reshape(1, b, s, *hd), vp.reshape(1, b, s, *hd),
            ks.reshape(1, db, t, *hd), vs.reshape(1, db, t, *hd),
            s_prompt[None], s_sample[None])
```

```python
import functools
import math

import jax
import jax.numpy as jnp
from jax import lax
from jax.experimental import pallas as pl
from jax.experimental.pallas import tpu as pltpu

F32 = jnp.float32
BF16 = jnp.bfloat16
I32 = jnp.int32

N_HEADS = 8
HEAD_DIM = 128
MOBA_BLOCK = 256
MOBA_TOPK = 3
PAGE_SIZE = 128
N_BUCKETS = 32
MAX_DISTANCE = 128
MAX_EXACT = N_BUCKETS // 2
HG_HEADS = 8
HG_DIM = 128
EPS = 1e-6

LANES = 128
SUBLANES = 8
VMEM_LIMIT = 56 * 1024 * 1024

NEG = -1e30
LOG2E = math.log2(math.e)

NT_DIMS = (((1,), (1,)), ((), ()))
TN_DIMS = (((0,), (0,)), ((), ()))


def _params(sem):
    return pltpu.CompilerParams(dimension_semantics=sem, vmem_limit_bytes=VMEM_LIMIT)


def _rms(x, g):
    ms = jnp.mean(x * x, axis=-1, keepdims=True)
    return x * lax.rsqrt(ms + EPS) * g


def _t5_bias(dist, rb_get):
    n = jnp.maximum(dist, 0)
    nf = jnp.maximum(n, 1).astype(F32)
    large = MAX_EXACT + (jnp.log(nf / MAX_EXACT) / math.log(MAX_DISTANCE / MAX_EXACT)
                         * (N_BUCKETS - MAX_EXACT)).astype(I32)
    large = jnp.minimum(large, N_BUCKETS - 1)
    bucket = jnp.where(n < MAX_EXACT, n, large)
    out = jnp.zeros(dist.shape, F32)
    for k in range(N_BUCKETS):
        out = jnp.where(bucket == k, rb_get(k), out)
    return out


def _qkv_body(x_ref, g_ref, w_ref, q_ref, k_ref, v_ref):
    d = x_ref.shape[1]
    xn = _rms(x_ref[...], g_ref[...]).astype(BF16)
    for c, o_ref in enumerate((q_ref, k_ref, v_ref)):
        o_ref[...] = jnp.dot(xn, w_ref[:, c * d:(c + 1) * d], preferred_element_type=F32)


def _qkv_proj(x, g, w_bf16, tm):
    m, d = x.shape
    row = pl.BlockSpec((tm, d), lambda i: (i, 0))
    return pl.pallas_call(
        _qkv_body,
        out_shape=[jax.ShapeDtypeStruct((m, d), F32)] * 3,
        grid=(m // tm,),
        in_specs=[row, pl.BlockSpec((1, d), lambda i: (0, 0)),
                  pl.BlockSpec((d, 3 * d), lambda i: (0, 0))],
        out_specs=[row] * 3,
        compiler_params=_params(("arbitrary",)),
        name="qkv_proj",
    )(x, g.reshape(1, d), w_bf16)


def _hgrn_proj_body(layer, x_ref, g_ref, w_ref, lbp_ref, q_ref, k_ref, v_ref, lf_ref, gs_ref):
    d = x_ref.shape[1]
    xn = _rms(x_ref[...], g_ref[...]).astype(BF16)

    def proj(c):
        return jnp.dot(xn, w_ref[:, c * d:(c + 1) * d], preferred_element_type=F32)

    lbp = lbp_ref[...]
    e = jnp.exp(lbp - jnp.max(lbp, axis=0, keepdims=True))
    sm = e / jnp.sum(e, axis=0, keepdims=True)
    cum = sm[0:1]
    for r in range(1, layer + 1):
        cum = cum + sm[r:r + 1]
    lb = cum - sm[0:1]

    q_pre = proj(0)
    q_ref[...] = q_pre * jax.nn.sigmoid(q_pre) * (HG_DIM ** -0.5)
    fgate = lb + (1.0 - lb) * jax.nn.sigmoid(proj(1))
    lf_ref[...] = jnp.log(fgate)
    k_ref[...] = 1.0 - fgate
    v_ref[...] = proj(2)
    g_pre = proj(3)
    gs_ref[...] = g_pre * jax.nn.sigmoid(g_pre)


def _hgrn_proj(x, g, w_bf16, lb_param, layer, tm):
    m, d = x.shape
    depth = lb_param.shape[0]
    row = pl.BlockSpec((tm, d), lambda i: (i, 0))
    return pl.pallas_call(
        functools.partial(_hgrn_proj_body, layer),
        out_shape=[jax.ShapeDtypeStruct((m, d), F32)] * 5,
        grid=(m // tm,),
        in_specs=[row, pl.BlockSpec((1, d), lambda i: (0, 0)),
                  pl.BlockSpec((d, 4 * d), lambda i: (0, 0)),
                  pl.BlockSpec((depth, d), lambda i: (0, 0))],
        out_specs=[row] * 5,
        compiler_params=_params(("arbitrary",)),
        name="hgrn_proj",
    )(x, g.reshape(1, d), w_bf16, lb_param)


def _proj_mlp_body(final_norm, x_ref, o_ref, wo_ref, g_ref, w1_ref, w2_ref, gf_ref,
                   out_ref, x1_scr, xn_scr):
    f = pl.program_id(1)

    @pl.when(f == 0)
    def _():
        x1 = x_ref[...] + jnp.dot(o_ref[...], wo_ref[...], preferred_element_type=F32)
        x1_scr[...] = x1
        xn_scr[...] = _rms(x1, g_ref[...]).astype(BF16)
        out_ref[...] = jnp.zeros_like(out_ref)

    h = jnp.maximum(jnp.dot(xn_scr[...], w1_ref[...], preferred_element_type=F32), 0.0)
    out_ref[...] += jnp.dot((h * h).astype(BF16), w2_ref[...], preferred_element_type=F32)

    @pl.when(f == pl.num_programs(1) - 1)
    def _():
        y = x1_scr[...] + out_ref[...]
        if final_norm:
            y = _rms(y, gf_ref[...])
        out_ref[...] = y


def _proj_mlp(x, o_bf16, wo, g_ffn, w1, w2, g_final, final_norm, tm, tf):
    m, d = x.shape
    dff = w1.shape[1]
    row = pl.BlockSpec((tm, d), lambda i, f: (i, 0))
    vec = pl.BlockSpec((1, d), lambda i, f: (0, 0))
    return pl.pallas_call(
        functools.partial(_proj_mlp_body, final_norm),
        out_shape=jax.ShapeDtypeStruct((m, d), F32),
        grid=(m // tm, dff // tf),
        in_specs=[row, row, pl.BlockSpec((d, d), lambda i, f: (0, 0)), vec,
                  pl.BlockSpec((d, tf), lambda i, f: (0, f)),
                  pl.BlockSpec((tf, d), lambda i, f: (f, 0)), vec],
        out_specs=row,
        scratch_shapes=[pltpu.VMEM((tm, d), F32), pltpu.VMEM((tm, d), BF16)],
        compiler_params=_params(("arbitrary", "arbitrary")),
        name="proj_mlp",
    )(x, o_bf16, wo, g_ffn.reshape(1, d), w1, w2, g_final.reshape(1, d))


def _bias_tables_body(rb_ref, own_ref, adj_ref, far_ref, last_ref):
    h = pl.program_id(0)
    shape = (MOBA_BLOCK, MOBA_BLOCK)
    d_own = lax.broadcasted_iota(I32, shape, 1) - lax.broadcasted_iota(I32, shape, 0)

    def rb_get(k):
        return rb_ref[h * N_BUCKETS + k] * LOG2E

    own_ref[0] = jnp.where(d_own >= 0, _t5_bias(d_own, rb_get), NEG)
    adj_ref[0] = _t5_bias(d_own + MOBA_BLOCK, rb_get)
    d_far = jnp.full((1, LANES), MOBA_BLOCK + 1, I32)
    far_ref[0] = _t5_bias(d_far, rb_get)
    lshape = (SUBLANES, MOBA_BLOCK)
    d_last = (MOBA_BLOCK + lax.broadcasted_iota(I32, lshape, 0)
              - lax.broadcasted_iota(I32, lshape, 1))
    last_ref[0] = _t5_bias(d_last, rb_get)


def _bias_tables(rb_flat):
    tile = pl.BlockSpec((1, MOBA_BLOCK, MOBA_BLOCK), lambda h: (h, 0, 0))
    return pl.pallas_call(
        _bias_tables_body,
        out_shape=[jax.ShapeDtypeStruct((N_HEADS, MOBA_BLOCK, MOBA_BLOCK), F32),
                   jax.ShapeDtypeStruct((N_HEADS, MOBA_BLOCK, MOBA_BLOCK), F32),
                   jax.ShapeDtypeStruct((N_HEADS, 1, LANES), F32),
                   jax.ShapeDtypeStruct((N_HEADS, SUBLANES, MOBA_BLOCK), F32)],
        grid=(N_HEADS,),
        in_specs=[pl.BlockSpec(memory_space=pltpu.SMEM)],
        out_specs=[tile, tile, pl.BlockSpec((1, 1, LANES), lambda h: (h, 0, 0)),
                   pl.BlockSpec((1, SUBLANES, MOBA_BLOCK), lambda h: (h, 0, 0))],
        compiler_params=_params(("arbitrary",)),
        name="bias_tables",
    )(rb_flat)


PAD_BLOCKS = 2
NB_MAX = 32
AUG = 2 * HEAD_DIM
FLAG_COL = 2 * NB_MAX
HEADS_PER_STEP = 4


MEAN_DEPTH = 4
MEAN_GROUP = 16


def _moba_prompt_body(layer, pt_ref, q_ref, k_ref, v_ref, own_ref, adj_ref, far_ref, ck_hbm,
                      o_ref, bm_hbm, kaug, vt, kmean, qaug, s_scr, p_scr, acc_s,
                      ring, stage, count, sem_in, sem_out):
    qi = pl.program_id(2)
    nb = kmean.shape[1]
    blk = MOBA_BLOCK
    heads = range(HEADS_PER_STEP)
    ppb = MOBA_BLOCK // PAGE_SIZE
    blocks_per_seq = pt_ref.shape[1] // ppb
    n_mean = pt_ref.shape[0] * blocks_per_seq
    grid_pos = (pl.program_id(0) * pl.num_programs(1) + pl.program_id(1)) * pl.num_programs(2) + qi
    n_steps = pl.num_programs(0) * pl.num_programs(1) * pl.num_programs(2)

    def hcols(i):
        return slice(i * HEAD_DIM, (i + 1) * HEAD_DIM)

    def page_copies(n):
        seq = n // blocks_per_seq
        first_page = (n - seq * blocks_per_seq) * ppb
        slot = n % MEAN_DEPTH
        return [pltpu.make_async_copy(ck_hbm.at[layer, pt_ref[seq, first_page + pg]],
                                      ring.at[slot, pg], sem_in.at[slot]) for pg in range(ppb)]

    def group_copy(g):
        half = g % 2
        return pltpu.make_async_copy(
            stage.at[half], bm_hbm.at[pl.ds(pl.multiple_of(g * MEAN_GROUP, MEAN_GROUP), MEAN_GROUP)],
            sem_out.at[half])

    def reduce_block(n):
        slot = n % MEAN_DEPTH
        tot = jnp.sum(ring[slot, 0], axis=0)
        for pg in range(1, ppb):
            tot = tot + jnp.sum(ring[slot, pg], axis=0)
        g = n // MEAN_GROUP
        stage[jnp.where(n < n_mean, g % 2, 2), n - g * MEAN_GROUP] = tot * (1.0 / MOBA_BLOCK)

    def advance_stream(n):
        @pl.when(n + MEAN_DEPTH < n_mean)
        def _():
            for cp in page_copies(n + MEAN_DEPTH):
                cp.start()

        @pl.when(n + 1 < n_mean)
        def _():
            for cp in page_copies(n + 1):
                cp.wait()

        g = n // MEAN_GROUP

        @pl.when((n - g * MEAN_GROUP == MEAN_GROUP - 1) & (n < n_mean))
        def _():
            @pl.when(g >= 1)
            def _():
                group_copy(g - 1).wait()
            group_copy(g).start()

    @pl.when(grid_pos == 0)
    def _():
        count[0] = 0
        for n in range(MEAN_DEPTH):
            for cp in page_copies(n):
                cp.start()
        for cp in page_copies(0):
            cp.wait()

    @pl.when(qi == 0)
    def _():
        lane_p = lax.broadcasted_iota(I32, (PAD_BLOCKS * blk, AUG), 1)
        pad_keys = jnp.where(lane_p == HEAD_DIM + FLAG_COL, 1.0, 0.0).astype(BF16)
        row_q = lax.broadcasted_iota(I32, (AUG - HEAD_DIM, blk), 0)
        for i in heads:
            kaug[i, 0:PAD_BLOCKS * blk, :] = pad_keys
            for n in range(PAD_BLOCKS):
                vt[i, n] = jnp.zeros((HEAD_DIM, blk), BF16)
            kmean[i] = jnp.zeros((nb, HEAD_DIM), F32)
            qaug[i, HEAD_DIM:AUG, :] = jnp.where(row_q == FLAG_COL, NEG, 0.0).astype(BF16)

    lane = lax.broadcasted_iota(I32, (blk, LANES), 1)
    hot = jnp.where(lane == qi, 1.0, jnp.where(lane == NB_MAX + qi, 1.0, 0.0)).astype(BF16)
    row_n = lax.broadcasted_iota(I32, (nb, HEAD_DIM), 0)
    own_rows = pl.ds(pl.multiple_of((qi + PAD_BLOCKS) * blk, blk), blk)
    for i in heads:
        kblk = k_ref[0, :, hcols(i)]
        kaug[i, own_rows, 0:HEAD_DIM] = kblk.astype(BF16)
        kaug[i, own_rows, HEAD_DIM:AUG] = hot
        vt[i, qi + PAD_BLOCKS] = v_ref[0, :, hcols(i)].T.astype(BF16)
        kmean[i] = jnp.where(row_n == qi, jnp.mean(kblk, axis=0, keepdims=True), kmean[i])

    qf = qi.astype(F32)

    def write_query_side(i):
        q = q_ref[0, :, hcols(i)]
        gate = lax.dot_general(kmean[i], q, NT_DIMS, precision=lax.Precision.HIGHEST,
                               preferred_element_type=F32)
        n_io = lax.broadcasted_iota(I32, gate.shape, 0).astype(F32)
        g = jnp.where(n_io < qf, gate, -jnp.inf)
        sel = jnp.zeros(gate.shape, F32)
        for _ in range(MOBA_TOPK):
            mx = jnp.max(g, axis=0, keepdims=True)
            idx = jnp.min(jnp.where(g == mx, n_io, float(nb)), axis=0, keepdims=True)
            hit = n_io == idx
            sel = jnp.where(hit, 1.0, sel)
            g = jnp.where(hit, -jnp.inf, g)
        far = jnp.concatenate([far_ref[i]] * (blk // LANES), axis=1)
        picked = jnp.where(sel > 0.0, jnp.where(n_io < qf - 1.0, far, 0.0), NEG)
        bias = jnp.where(n_io < qf, picked, jnp.where(n_io == qf, 0.0, NEG))
        bias_hi = bias.astype(BF16)
        qaug[i, 0:HEAD_DIM, :] = (q * (HEAD_DIM ** -0.5 * LOG2E)).T.astype(BF16)
        qaug[i, HEAD_DIM:HEAD_DIM + nb, :] = bias_hi
        qaug[i, HEAD_DIM + NB_MAX:HEAD_DIM + NB_MAX + nb, :] = (
            bias - bias_hi.astype(F32)).astype(BF16)

    def scores(i, padded_block):
        r0 = pl.multiple_of(padded_block * blk, blk)
        return jnp.dot(kaug[i, pl.ds(r0, blk), :], qaug[i], preferred_element_type=F32)

    for i in heads:
        write_query_side(i)
        s_scr[i] = scores(i, qi + PAD_BLOCKS) + own_ref[i]
        acc_s[i] = jnp.zeros((HEAD_DIM, blk), F32)

    carry0 = []
    for i in heads:
        s_next = scores(i, qi - 1 + PAD_BLOCKS) + adj_ref[i]
        s_cur = s_scr[i]
        m0 = jnp.max(s_cur, axis=0, keepdims=True)
        p = jnp.exp2(s_cur - m0)
        p_scr[i] = p.astype(BF16)
        s_scr[i] = s_next
        carry0 += [m0, jnp.sum(p, axis=0, keepdims=True), jnp.ones_like(m0)]

    def step(t, carry):
        n = carry[-1]
        reduce_block(n)
        out = []
        for i in heads:
            m_prev, l_prev, alpha_prev = carry[3 * i:3 * i + 3]
            s_next = scores(i, qi - t - 1 + PAD_BLOCKS)
            pv = jnp.dot(vt[i, qi - t + 1 + PAD_BLOCKS], p_scr[i], preferred_element_type=F32)
            acc_s[i] = alpha_prev * acc_s[i] + pv
            s_cur = s_scr[i]
            m_new = jnp.maximum(m_prev, jnp.max(s_cur, axis=0, keepdims=True))
            alpha = jnp.exp2(m_prev - m_new)
            p = jnp.exp2(s_cur - m_new)
            p_scr[i] = p.astype(BF16)
            s_scr[i] = s_next
            out += [m_new, alpha * l_prev + jnp.sum(p, axis=0, keepdims=True), alpha]
        advance_stream(n)
        return tuple(out) + (n + 1,)

    fin = lax.fori_loop(1, qi + 2, step, tuple(carry0) + (count[0],))
    for i in heads:
        o_ref[0, :, hcols(i)] = (acc_s[i] / fin[3 * i + 1]).T.astype(o_ref.dtype)
    count[0] = fin[-1]

    @pl.when(grid_pos == n_steps - 1)
    def _():
        def drain(n, carry):
            reduce_block(n)
            advance_stream(n)
            return carry

        lax.fori_loop(fin[-1], n_mean, drain, 0)
        group_copy(n_mean // MEAN_GROUP - 1).wait()


def _moba_prompt(q, k, v, own, adj, far, cache_k, page_table, layer):
    b, s, d = q.shape
    nb = s // MOBA_BLOCK
    assert nb <= NB_MAX and FLAG_COL < AUG - HEAD_DIM and N_HEADS % HEADS_PER_STEP == 0
    blk = MOBA_BLOCK
    hp = HEADS_PER_STEP
    ppb = MOBA_BLOCK // PAGE_SIZE
    db, n_pages = page_table.shape
    n_mean = db * (n_pages // ppb)
    assert n_pages % ppb == 0 and n_mean % MEAN_GROUP == 0 and n_mean >= MEAN_DEPTH
    qspec = pl.BlockSpec((1, blk, hp * HEAD_DIM), lambda bi, h, qi, pt: (bi, qi, h))
    kvspec = qspec
    tile = pl.BlockSpec((hp, blk, blk), lambda bi, h, qi, pt: (h, 0, 0))
    anyspec = pl.BlockSpec(memory_space=pl.ANY)
    out, means = pl.pallas_call(
        functools.partial(_moba_prompt_body, layer),
        out_shape=[jax.ShapeDtypeStruct((b, s, d), BF16),
                   jax.ShapeDtypeStruct((n_mean, N_HEADS, HEAD_DIM), F32)],
        grid_spec=pltpu.PrefetchScalarGridSpec(
            num_scalar_prefetch=1,
            grid=(b, N_HEADS // hp, nb),
            in_specs=[qspec, kvspec, kvspec, tile, tile,
                      pl.BlockSpec((hp, 1, LANES), lambda bi, h, qi, pt: (h, 0, 0)), anyspec],
            out_specs=[qspec, anyspec],
            scratch_shapes=[pltpu.VMEM((hp, (nb + PAD_BLOCKS) * blk, AUG), BF16),
                            pltpu.VMEM((hp, nb + PAD_BLOCKS, HEAD_DIM, blk), BF16),
                            pltpu.VMEM((hp, nb, HEAD_DIM), F32),
                            pltpu.VMEM((hp, AUG, blk), BF16),
                            pltpu.VMEM((hp, blk, blk), F32),
                            pltpu.VMEM((hp, blk, blk), BF16),
                            pltpu.VMEM((hp, HEAD_DIM, blk), F32),
                            pltpu.VMEM((MEAN_DEPTH, ppb, PAGE_SIZE, N_HEADS, HEAD_DIM), F32),
                            pltpu.VMEM((3, MEAN_GROUP, N_HEADS, HEAD_DIM), F32),
                            pltpu.SMEM((1,), I32),
                            pltpu.SemaphoreType.DMA((MEAN_DEPTH,)),
                            pltpu.SemaphoreType.DMA((2,))]),
        compiler_params=_params(("arbitrary", "arbitrary", "arbitrary")),
        name="moba_prompt",
    )(page_table, q, k, v, own, adj, far, cache_k)
    return out, means.reshape(db, n_pages // ppb, N_HEADS, HEAD_DIM)


def _sample_select_body(q_ref, bm_ref, sel_ref):
    nblk = bm_ref.shape[1] // N_HEADS
    q = q_ref[0]
    for h in range(N_HEADS):
        qh = q[:, h * HEAD_DIM:(h + 1) * HEAD_DIM]
        bmh = bm_ref[0, pl.ds(h, nblk, stride=N_HEADS), :]
        g = lax.dot_general(qh, bmh, NT_DIMS, precision=lax.Precision.HIGHEST,
                            preferred_element_type=F32)
        n_iota = lax.broadcasted_iota(I32, g.shape, 1).astype(F32)
        lane = lax.broadcasted_iota(I32, (g.shape[0], LANES), 1)
        out = jnp.zeros((g.shape[0], LANES), I32)
        for r in range(MOBA_TOPK):
            mx = jnp.max(g, axis=1, keepdims=True)
            idx = jnp.min(jnp.where(g == mx, n_iota, float(nblk)), axis=1, keepdims=True)
            out = jnp.where(lane == r, idx.astype(I32), out)
            g = jnp.where(n_iota == idx, -jnp.inf, g)
        sel_ref[0, h] = out


def _sample_select(q3, blk_mean):
    db, t, d = q3.shape
    nblk = blk_mean.shape[1]
    bm2 = blk_mean.reshape(db, nblk * N_HEADS, HEAD_DIM)
    return pl.pallas_call(
        _sample_select_body,
        out_shape=jax.ShapeDtypeStruct((db, N_HEADS, t, LANES), I32),
        grid=(db,),
        in_specs=[pl.BlockSpec((1, t, d), lambda b: (b, 0, 0)),
                  pl.BlockSpec((1, nblk * N_HEADS, HEAD_DIM), lambda b: (b, 0, 0))],
        out_specs=pl.BlockSpec((1, N_HEADS, t, LANES), lambda b: (b, 0, 0, 0)),
        compiler_params=_params(("arbitrary",)),
        name="sample_select",
    )(q3, bm2)


def _sample_attn_body(layer, past_len, pt_ref, sel_ref, q_ref, kn_ref, vn_ref, far_ref, last_ref,
                      rb_ref, ck_hbm, cv_hbm, o_ref, kbuf, vbuf, sem):
    bi = pl.program_id(0)
    h = pl.program_id(1)
    nh = pl.num_programs(1)
    t_len = q_ref.shape[1]
    ppb = MOBA_BLOCK // PAGE_SIZE
    n_sel = t_len * MOBA_TOPK
    step = bi * nh + h
    slot = step % 2

    def sel_at(b_idx, h_idx, j):
        return sel_ref[(b_idx * nh + h_idx) * n_sel + j]

    def copies(b_idx, h_idx, sl):
        out = []
        for j in range(n_sel):
            blk_id = sel_at(b_idx, h_idx, j)
            for pg in range(ppb):
                page = pt_ref[b_idx, blk_id * ppb + pg]
                dst = pl.ds(j * MOBA_BLOCK + pg * PAGE_SIZE, PAGE_SIZE)
                out.append(pltpu.make_async_copy(
                    ck_hbm.at[layer, page, :, h_idx, :], kbuf.at[sl, dst, :], sem.at[0, sl]))
                out.append(pltpu.make_async_copy(
                    cv_hbm.at[layer, page, :, h_idx, :], vbuf.at[sl, dst, :], sem.at[1, sl]))
        return out

    @pl.when(step == 0)
    def _():
        for cp in copies(0, 0, 0):
            cp.start()

    @pl.when(step + 1 < pl.num_programs(0) * nh)
    def _():
        nxt = step + 1
        for cp in copies(nxt // nh, nxt % nh, 1 - slot):
            cp.start()

    def rb_get(k):
        return rb_ref[h * N_BUCKETS + k] * LOG2E

    q = q_ref[0] * (HEAD_DIM ** -0.5 * LOG2E)
    kn = kn_ref[0]
    vn = vn_ref[0]
    n_cols = n_sel * MOBA_BLOCK
    col = lax.broadcasted_iota(I32, (t_len, n_cols), 1)
    row = lax.broadcasted_iota(I32, (t_len, n_cols), 0)
    far = jnp.broadcast_to(jnp.concatenate([far_ref[0]] * (MOBA_BLOCK // LANES), axis=1),
                           (t_len, MOBA_BLOCK))
    last = last_ref[0, 0:t_len, :]
    last_block = past_len // MOBA_BLOCK - 1
    bias = jnp.concatenate(
        [jnp.where(sel_at(bi, h, j) == last_block, last, far) for j in range(n_sel)], axis=1)
    per_q = MOBA_TOPK * MOBA_BLOCK
    mine = (col >= row * per_q) & (col < (row + 1) * per_q)
    bias = jnp.where(mine, bias, NEG)

    row1 = lax.broadcasted_iota(I32, (t_len, LANES), 0)
    s_own = []
    for j in range(t_len):
        d_o = row1 - j
        s_j = jnp.sum(q * kn[j:j + 1, :], axis=1, keepdims=True)
        s_own.append(jnp.where(d_o >= 0, s_j + _t5_bias(d_o, rb_get), NEG)[:, 0:1])

    for cp in copies(bi, h, slot):
        cp.wait()
    s = lax.dot_general(q.astype(BF16), kbuf[slot].astype(BF16), NT_DIMS,
                        preferred_element_type=F32) + bias
    m = jnp.max(s, axis=1, keepdims=True)
    for s_j in s_own:
        m = jnp.maximum(m, s_j)
    p = jnp.exp2(s - m)
    den = jnp.sum(p, axis=1, keepdims=True)
    acc = jnp.dot(p.astype(BF16), vbuf[slot].astype(BF16), preferred_element_type=F32)
    for j, s_j in enumerate(s_own):
        p_j = jnp.exp2(s_j - m)
        den = den + p_j
        acc = acc + p_j * vn[j:j + 1, :]
    o_ref[0] = acc / den


def _sample_attn(q3, k3, v3, far, last, cache_k, cache_v, page_table, sel_flat, rb_flat, layer):
    db, t, d = q3.shape
    past_len = page_table.shape[1] * PAGE_SIZE
    assert past_len % MOBA_BLOCK == 0 and t <= SUBLANES
    n_cols = t * MOBA_TOPK * MOBA_BLOCK
    spec = pl.BlockSpec((1, t, HEAD_DIM), lambda b, h, pt, sl: (b, 0, h))
    farspec = pl.BlockSpec((1, 1, LANES), lambda b, h, pt, sl: (h, 0, 0))
    lastspec = pl.BlockSpec((1, SUBLANES, MOBA_BLOCK), lambda b, h, pt, sl: (h, 0, 0))
    anyspec = pl.BlockSpec(memory_space=pl.ANY)
    return pl.pallas_call(
        functools.partial(_sample_attn_body, layer, past_len),
        out_shape=jax.ShapeDtypeStruct((db, t, d), F32),
        grid_spec=pltpu.PrefetchScalarGridSpec(
            num_scalar_prefetch=2,
            grid=(db, N_HEADS),
            in_specs=[spec, spec, spec, farspec, lastspec,
                      pl.BlockSpec(memory_space=pltpu.SMEM), anyspec, anyspec],
            out_specs=spec,
            scratch_shapes=[pltpu.VMEM((2, n_cols, HEAD_DIM), F32),
                            pltpu.VMEM((2, n_cols, HEAD_DIM), F32),
                            pltpu.SemaphoreType.DMA((2, 2))]),
        compiler_params=_params(("arbitrary", "arbitrary")),
        name="sample_attn",
    )(page_table, sel_flat, q3, k3, v3, far, last, rb_flat, cache_k, cache_v)


SUB = 8


def _gla_body(q_ref, k_ref, v_ref, lf_ref, gs_ref, gn_ref, s0_ref, o_ref, sfin_ref,
              st_scr, b_scr):
    ci = pl.program_id(1)
    chunk = q_ref.shape[0]
    nsub = chunk // SUB

    @pl.when(ci == 0)
    def _():
        for h in range(HG_HEADS):
            st_scr[h] = s0_ref[0, h].T

    r_i = lax.broadcasted_iota(I32, (chunk, chunk), 0)
    c_i = lax.broadcasted_iota(I32, (chunk, chunk), 1)
    tri = jnp.where(r_i >= c_i, 1.0, 0.0).astype(F32)
    b_all = jnp.dot(tri, lf_ref[...], precision=lax.Precision.HIGHEST,
                    preferred_element_type=F32) * LOG2E
    b_scr[...] = b_all
    neg_inf = -jnp.inf
    t_sub = lax.broadcasted_iota(I32, (SUB, HG_DIM), 0)
    causal = [t_sub >= s for s in range(SUB)]

    def zeros(n):
        return [jnp.zeros((n, HG_DIM), F32)] if n else []

    def cols(h):
        return slice(h * HG_DIM, (h + 1) * HG_DIM)


    o_state, att = [], []
    for h in range(HG_HEADS):
        sl = cols(h)
        b, q, k = b_all[:, sl], q_ref[:, sl], k_ref[:, sl]
        o_state.append(lax.dot_general((q * jnp.exp2(b)).astype(BF16),
                                       st_scr[h].astype(BF16), NT_DIMS,
                                       preferred_element_type=F32))
        if nsub > 1:
            qs, ks = [], []
            for j in range(nsub - 1):
                lo, hi = SUB * j, SUB * (j + 1)
                e_j = b[hi - 1:hi, :]
                qs.append(jnp.concatenate(
                    zeros(hi) + [q[hi:] * jnp.exp2(b[hi:] - e_j)], axis=0))
                ks.append(jnp.concatenate(
                    zeros(lo) + [k[lo:hi] * jnp.exp2(e_j - b[lo:hi])] + zeros(chunk - hi), axis=0))
            att.append(lax.dot_general(jnp.concatenate(qs, axis=1).astype(BF16),
                                       jnp.concatenate(ks, axis=1).astype(BF16), NT_DIMS,
                                       preferred_element_type=F32))

    for h in range(HG_HEADS):
        sl = cols(h)
        b = b_all[:, sl]
        b_last = b[chunk - 1:chunk, :]
        ke = k_ref[:, sl] * jnp.exp2(b_last - b)
        st_scr[h] = st_scr[h] * jnp.exp2(b_last) + lax.dot_general(
            v_ref[:, sl].astype(BF16), ke.astype(BF16), TN_DIMS, preferred_element_type=F32)

    o_diag = []
    for h in range(HG_HEADS):
        sl = cols(h)
        diag = []
        for i in range(nsub):
            r0 = SUB * i
            b_i, q_i = b_all[r0:r0 + SUB, sl], q_ref[r0:r0 + SUB, sl]
            o_i = jnp.zeros((SUB, HG_DIM), F32)
            for s in range(SUB):
                r = r0 + s
                dec = jnp.exp2(jnp.where(causal[s], b_i - b_scr[r:r + 1, sl], neg_inf))
                a = jnp.sum(dec * q_i * k_ref[r:r + 1, sl], axis=1, keepdims=True)
                o_i = o_i + a * v_ref[r:r + 1, sl]
            diag.append(o_i)
        o_diag.append(jnp.concatenate(diag, axis=0) if nsub > 1 else diag[0])

    for h in range(HG_HEADS):
        sl = cols(h)
        o = o_state[h] + o_diag[h]
        if nsub > 1:
            o = o + jnp.dot(att[h].astype(BF16), v_ref[:, sl].astype(BF16),
                            preferred_element_type=F32)
        o = o * lax.rsqrt(jnp.mean(o * o, axis=-1, keepdims=True) + EPS)
        o_ref[:, sl] = (o * gn_ref[:, sl] * gs_ref[:, sl]).astype(o_ref.dtype)

    @pl.when(ci == pl.num_programs(1) - 1)
    def _():
        for h in range(HG_HEADS):
            sfin_ref[0, h] = st_scr[h].T


def _gla(q, k, v, lf, gs, g_norm, s0, chunk):
    m, d = q.shape
    nb = s0.shape[0]
    nc = m // nb // chunk
    row = pl.BlockSpec((chunk, d), lambda b, c: (b * nc + c, 0))
    state = pl.BlockSpec((1, HG_HEADS, HG_DIM, HG_DIM), lambda b, c: (b, 0, 0, 0))
    return pl.pallas_call(
        _gla_body,
        out_shape=[jax.ShapeDtypeStruct((m, d), BF16),
                   jax.ShapeDtypeStruct(s0.shape, F32)],
        grid=(nb, nc),
        in_specs=[row, row, row, row, row, pl.BlockSpec((1, d), lambda b, c: (0, 0)), state],
        out_specs=[row, state],
        scratch_shapes=[pltpu.VMEM((HG_HEADS, HG_DIM, HG_DIM), F32),
                        pltpu.VMEM((chunk, d), F32)],
        compiler_params=_params(("arbitrary", "arbitrary")),
        name="gla",
    )(q, k, v, lf, gs, g_norm.reshape(1, d), s0)


PROMPT_TM = 512
MLP_TM = 1024
MLP_TF = 512
GLA_CHUNK = 64
GLA_SAMPLE_CHUNK = 16


def kernel(x_prompt, x_sample, cache_k, cache_v, state_hgrn, page_table, norm_mix, norm_ffn,
           norm_final, attn_w_qkv, attn_w_o, rel_bias, hg_w_in, hg_lb_param, hg_norm, hg_w_o,
           ffn_w1, ffn_w2):
    b, s, d = x_prompt.shape
    db, t, _ = x_sample.shape
    mp, ms = b * s, db * t
    xp = x_prompt.reshape(mp, d)
    xs = x_sample.reshape(ms, d)
    rb_flat = rel_bias.T.reshape(-1)
    w_qkv = attn_w_qkv.astype(BF16)
    w_ao = attn_w_o.astype(BF16)
    w_in = hg_w_in.astype(BF16)
    w_ho = hg_w_o.astype(BF16)
    w1 = ffn_w1.astype(BF16)
    w2 = ffn_w2.astype(BF16)

    own, adj, far, last = _bias_tables(rb_flat)
    qp, kp, vp = _qkv_proj(xp, norm_mix[0], w_qkv[0], PROMPT_TM)
    op, blk_mean = _moba_prompt(qp.reshape(b, s, d), kp.reshape(b, s, d), vp.reshape(b, s, d),
                                own, adj, far, cache_k, page_table, 0)
    xp = _proj_mlp(xp, op.reshape(mp, d), w_ao[0], norm_ffn[0], w1[0], w2[0], norm_final,
                   False, MLP_TM, MLP_TF)

    qs, ks, vs = _qkv_proj(xs, norm_mix[0], w_qkv[0], ms)
    q3 = qs.reshape(db, t, d)
    sel = _sample_select(q3, blk_mean)
    sel_flat = sel[:, :, :, :MOBA_TOPK].reshape(-1)
    os_ = _sample_attn(q3, ks.reshape(db, t, d), vs.reshape(db, t, d), far, last,
                       cache_k, cache_v, page_table, sel_flat, rb_flat, 0)
    xs = _proj_mlp(xs, os_.reshape(ms, d).astype(BF16), w_ao[0], norm_ffn[0], w1[0], w2[0],
                   norm_final, False, ms, MLP_TF)

    hq, hk, hv, hlf, hgs = _hgrn_proj(xp, norm_mix[1], w_in[0], hg_lb_param, 1, PROMPT_TM)
    s0p = jnp.zeros((b, HG_HEADS, HG_DIM, HG_DIM), F32)
    hop, s_prompt = _gla(hq, hk, hv, hlf, hgs, hg_norm[0], s0p, GLA_CHUNK)
    y_prompt = _proj_mlp(xp, hop, w_ho[0], norm_ffn[1], w1[1], w2[1], norm_final,
                         True, MLP_TM, MLP_TF)

    outs = _hgrn_proj(xs, norm_mix[1], w_in[0], hg_lb_param, 1, ms)
    pad = GLA_SAMPLE_CHUNK - t
    padded = [jnp.pad(a.reshape(db, t, d), ((0, 0), (0, pad), (0, 0))).reshape(
        db * GLA_SAMPLE_CHUNK, d) for a in outs]
    hos, s_sample = _gla(*padded, hg_norm[0], state_hgrn[0], GLA_SAMPLE_CHUNK)
    hos = hos.reshape(db, GLA_SAMPLE_CHUNK, d)[:, :t].reshape(ms, d)
    y_sample = _proj_mlp(xs, hos, w_ho[0], norm_ffn[1], w1[1], w2[1], norm_final,
                         True, ms, MLP_TF)

    hd = (N_HEADS, HEAD_DIM)
    return (y_prompt.reshape(b, s, d), y_sample.reshape(db, t, d),
            kp.reshape(1, b, s, *hd), vp.reshape(1, b, s, *hd),
            ks.reshape(1, db, t, *hd), vs.reshape(1, db, t, *hd),
            s_prompt[None], s_sample[None])
```

```python
import functools
import math

import jax
import jax.numpy as jnp
from jax import lax
from jax.experimental import pallas as pl
from jax.experimental.pallas import tpu as pltpu

F32 = jnp.float32
BF16 = jnp.bfloat16
I32 = jnp.int32

N_HEADS = 8
HEAD_DIM = 128
MOBA_BLOCK = 256
MOBA_TOPK = 3
PAGE_SIZE = 128
N_BUCKETS = 32
MAX_DISTANCE = 128
MAX_EXACT = N_BUCKETS // 2
HG_HEADS = 8
HG_DIM = 128
EPS = 1e-6

LANES = 128
SUBLANES = 8
VMEM_LIMIT = 56 * 1024 * 1024

NEG = -1e30
LOG2E = math.log2(math.e)

NT_DIMS = (((1,), (1,)), ((), ()))
TN_DIMS = (((0,), (0,)), ((), ()))


def _params(sem):
    return pltpu.CompilerParams(dimension_semantics=sem, vmem_limit_bytes=VMEM_LIMIT)


def _rms(x, g):
    ms = jnp.mean(x * x, axis=-1, keepdims=True)
    return x * lax.rsqrt(ms + EPS) * g


def _t5_bias(dist, rb_get):
    n = jnp.maximum(dist, 0)
    nf = jnp.maximum(n, 1).astype(F32)
    large = MAX_EXACT + (jnp.log(nf / MAX_EXACT) / math.log(MAX_DISTANCE / MAX_EXACT)
                         * (N_BUCKETS - MAX_EXACT)).astype(I32)
    large = jnp.minimum(large, N_BUCKETS - 1)
    bucket = jnp.where(n < MAX_EXACT, n, large)
    out = jnp.zeros(dist.shape, F32)
    for k in range(N_BUCKETS):
        out = jnp.where(bucket == k, rb_get(k), out)
    return out


def _qkv_body(x_ref, g_ref, w_ref, q_ref, k_ref, v_ref):
    d = x_ref.shape[1]
    xn = _rms(x_ref[...], g_ref[...]).astype(BF16)
    for c, o_ref in enumerate((q_ref, k_ref, v_ref)):
        o_ref[...] = jnp.dot(xn, w_ref[:, c * d:(c + 1) * d], preferred_element_type=F32)


def _qkv_proj(x, g, w_bf16, tm):
    m, d = x.shape
    row = pl.BlockSpec((tm, d), lambda i: (i, 0))
    return pl.pallas_call(
        _qkv_body,
        out_shape=[jax.ShapeDtypeStruct((m, d), F32)] * 3,
        grid=(m // tm,),
        in_specs=[row, pl.BlockSpec((1, d), lambda i: (0, 0)),
                  pl.BlockSpec((d, 3 * d), lambda i: (0, 0))],
        out_specs=[row] * 3,
        compiler_params=_params(("arbitrary",)),
        name="qkv_proj",
    )(x, g.reshape(1, d), w_bf16)


def _hgrn_proj_body(layer, x_ref, g_ref, w_ref, lbp_ref, q_ref, k_ref, v_ref, lf_ref, gs_ref):
    d = x_ref.shape[1]
    xn = _rms(x_ref[...], g_ref[...]).astype(BF16)

    def proj(c):
        return jnp.dot(xn, w_ref[:, c * d:(c + 1) * d], preferred_element_type=F32)

    lbp = lbp_ref[...]
    e = jnp.exp(lbp - jnp.max(lbp, axis=0, keepdims=True))
    sm = e / jnp.sum(e, axis=0, keepdims=True)
    cum = sm[0:1]
    for r in range(1, layer + 1):
        cum = cum + sm[r:r + 1]
    lb = cum - sm[0:1]

    q_pre = proj(0)
    q_ref[...] = q_pre * jax.nn.sigmoid(q_pre) * (HG_DIM ** -0.5)
    fgate = lb + (1.0 - lb) * jax.nn.sigmoid(proj(1))
    lf_ref[...] = jnp.log(fgate)
    k_ref[...] = 1.0 - fgate
    v_ref[...] = proj(2)
    g_pre = proj(3)
    gs_ref[...] = g_pre * jax.nn.sigmoid(g_pre)


def _hgrn_proj(x, g, w_bf16, lb_param, layer, tm):
    m, d = x.shape
    depth = lb_param.shape[0]
    row = pl.BlockSpec((tm, d), lambda i: (i, 0))
    return pl.pallas_call(
        functools.partial(_hgrn_proj_body, layer),
        out_shape=[jax.ShapeDtypeStruct((m, d), F32)] * 5,
        grid=(m // tm,),
        in_specs=[row, pl.BlockSpec((1, d), lambda i: (0, 0)),
                  pl.BlockSpec((d, 4 * d), lambda i: (0, 0)),
                  pl.BlockSpec((depth, d), lambda i: (0, 0))],
        out_specs=[row] * 5,
        compiler_params=_params(("arbitrary",)),
        name="hgrn_proj",
    )(x, g.reshape(1, d), w_bf16, lb_param)


def _proj_mlp_body(final_norm, x_ref, o_ref, wo_ref, g_ref, w1_ref, w2_ref, gf_ref,
                   out_ref, x1_scr, xn_scr):
    f = pl.program_id(1)

    @pl.when(f == 0)
    def _():
        x1 = x_ref[...] + jnp.dot(o_ref[...], wo_ref[...], preferred_element_type=F32)
        x1_scr[...] = x1
        xn_scr[...] = _rms(x1, g_ref[...]).astype(BF16)
        out_ref[...] = jnp.zeros_like(out_ref)

    h = jnp.maximum(jnp.dot(xn_scr[...], w1_ref[...], preferred_element_type=F32), 0.0)
    out_ref[...] += jnp.dot((h * h).astype(BF16), w2_ref[...], preferred_element_type=F32)

    @pl.when(f == pl.num_programs(1) - 1)
    def _():
        y = x1_scr[...] + out_ref[...]
        if final_norm:
            y = _rms(y, gf_ref[...])
        out_ref[...] = y


def _proj_mlp(x, o_bf16, wo, g_ffn, w1, w2, g_final, final_norm, tm, tf):
    m, d = x.shape
    dff = w1.shape[1]
    row = pl.BlockSpec((tm, d), lambda i, f: (i, 0))
    vec = pl.BlockSpec((1, d), lambda i, f: (0, 0))
    return pl.pallas_call(
        functools.partial(_proj_mlp_body, final_norm),
        out_shape=jax.ShapeDtypeStruct((m, d), F32),
        grid=(m // tm, dff // tf),
        in_specs=[row, row, pl.BlockSpec((d, d), lambda i, f: (0, 0)), vec,
                  pl.BlockSpec((d, tf), lambda i, f: (0, f)),
                  pl.BlockSpec((tf, d), lambda i, f: (f, 0)), vec],
        out_specs=row,
        scratch_shapes=[pltpu.VMEM((tm, d), F32), pltpu.VMEM((tm, d), BF16)],
        compiler_params=_params(("arbitrary", "arbitrary")),
        name="proj_mlp",
    )(x, o_bf16, wo, g_ffn.reshape(1, d), w1, w2, g_final.reshape(1, d))


def _bias_tables_body(rb_ref, own_ref, adj_ref, far_ref, last_ref):
    h = pl.program_id(0)
    shape = (MOBA_BLOCK, MOBA_BLOCK)
    d_own = lax.broadcasted_iota(I32, shape, 1) - lax.broadcasted_iota(I32, shape, 0)

    def rb_get(k):
        return rb_ref[h * N_BUCKETS + k] * LOG2E

    own_ref[0] = jnp.where(d_own >= 0, _t5_bias(d_own, rb_get), NEG)
    adj_ref[0] = _t5_bias(d_own + MOBA_BLOCK, rb_get)
    d_far = jnp.full((1, LANES), MOBA_BLOCK + 1, I32)
    far_ref[0] = _t5_bias(d_far, rb_get)
    lshape = (SUBLANES, MOBA_BLOCK)
    d_last = (MOBA_BLOCK + lax.broadcasted_iota(I32, lshape, 0)
              - lax.broadcasted_iota(I32, lshape, 1))
    last_ref[0] = _t5_bias(d_last, rb_get)


def _bias_tables(rb_flat):
    tile = pl.BlockSpec((1, MOBA_BLOCK, MOBA_BLOCK), lambda h: (h, 0, 0))
    return pl.pallas_call(
        _bias_tables_body,
        out_shape=[jax.ShapeDtypeStruct((N_HEADS, MOBA_BLOCK, MOBA_BLOCK), F32),
                   jax.ShapeDtypeStruct((N_HEADS, MOBA_BLOCK, MOBA_BLOCK), F32),
                   jax.ShapeDtypeStruct((N_HEADS, 1, LANES), F32),
                   jax.ShapeDtypeStruct((N_HEADS, SUBLANES, MOBA_BLOCK), F32)],
        grid=(N_HEADS,),
        in_specs=[pl.BlockSpec(memory_space=pltpu.SMEM)],
        out_specs=[tile, tile, pl.BlockSpec((1, 1, LANES), lambda h: (h, 0, 0)),
                   pl.BlockSpec((1, SUBLANES, MOBA_BLOCK), lambda h: (h, 0, 0))],
        compiler_params=_params(("arbitrary",)),
        name="bias_tables",
    )(rb_flat)


PAD_BLOCKS = 2
NB_MAX = 32
AUG = 2 * HEAD_DIM
FLAG_COL = 2 * NB_MAX
HEADS_PER_STEP = 4


MEAN_DEPTH = 8
MEAN_GROUP = 16


def _moba_prompt_body(layer, pt_ref, q_ref, k_ref, v_ref, own_ref, adj_ref, far_ref, ck_hbm,
                      o_ref, bm_hbm, kaug, vt, kmean, qaug, s_scr, p_scr, acc_s,
                      ring, stage, count, sem_in, sem_out):
    qi = pl.program_id(2)
    nb = kmean.shape[1]
    blk = MOBA_BLOCK
    heads = range(HEADS_PER_STEP)
    ppb = MOBA_BLOCK // PAGE_SIZE
    blocks_per_seq = pt_ref.shape[1] // ppb
    n_mean = pt_ref.shape[0] * blocks_per_seq
    grid_pos = (pl.program_id(0) * pl.num_programs(1) + pl.program_id(1)) * pl.num_programs(2) + qi
    n_steps = pl.num_programs(0) * pl.num_programs(1) * pl.num_programs(2)

    def hcols(i):
        return slice(i * HEAD_DIM, (i + 1) * HEAD_DIM)

    def page_copies(n):
        seq = n // blocks_per_seq
        first_page = (n - seq * blocks_per_seq) * ppb
        slot = n % MEAN_DEPTH
        return [pltpu.make_async_copy(ck_hbm.at[layer, pt_ref[seq, first_page + pg]],
                                      ring.at[slot, pg], sem_in.at[slot]) for pg in range(ppb)]

    def group_copy(g):
        half = g % 2
        return pltpu.make_async_copy(
            stage.at[half], bm_hbm.at[pl.ds(pl.multiple_of(g * MEAN_GROUP, MEAN_GROUP), MEAN_GROUP)],
            sem_out.at[half])

    def reduce_block(n):
        slot = n % MEAN_DEPTH
        tot = jnp.sum(ring[slot, 0], axis=0)
        for pg in range(1, ppb):
            tot = tot + jnp.sum(ring[slot, pg], axis=0)
        g = n // MEAN_GROUP
        stage[jnp.where(n < n_mean, g % 2, 2), n - g * MEAN_GROUP] = tot * (1.0 / MOBA_BLOCK)

    def advance_stream(n):
        @pl.when(n + MEAN_DEPTH < n_mean)
        def _():
            for cp in page_copies(n + MEAN_DEPTH):
                cp.start()

        @pl.when(n + 1 < n_mean)
        def _():
            for cp in page_copies(n + 1):
                cp.wait()

        g = n // MEAN_GROUP

        @pl.when((n - g * MEAN_GROUP == MEAN_GROUP - 1) & (n < n_mean))
        def _():
            @pl.when(g >= 1)
            def _():
                group_copy(g - 1).wait()
            group_copy(g).start()

    @pl.when(grid_pos == 0)
    def _():
        count[0] = 0
        for n in range(MEAN_DEPTH):
            for cp in page_copies(n):
                cp.start()
        for cp in page_copies(0):
            cp.wait()

    @pl.when(qi == 0)
    def _():
        lane_p = lax.broadcasted_iota(I32, (PAD_BLOCKS * blk, AUG), 1)
        pad_keys = jnp.where(lane_p == HEAD_DIM + FLAG_COL, 1.0, 0.0).astype(BF16)
        row_q = lax.broadcasted_iota(I32, (AUG - HEAD_DIM, blk), 0)
        for i in heads:
            kaug[i, 0:PAD_BLOCKS * blk, :] = pad_keys
            for n in range(PAD_BLOCKS):
                vt[i, n] = jnp.zeros((HEAD_DIM, blk), BF16)
            kmean[i] = jnp.zeros((nb, HEAD_DIM), F32)
            qaug[i, HEAD_DIM:AUG, :] = jnp.where(row_q == FLAG_COL, NEG, 0.0).astype(BF16)

    lane = lax.broadcasted_iota(I32, (blk, LANES), 1)
    hot = jnp.where(lane == qi, 1.0, jnp.where(lane == NB_MAX + qi, 1.0, 0.0)).astype(BF16)
    row_n = lax.broadcasted_iota(I32, (nb, HEAD_DIM), 0)
    own_rows = pl.ds(pl.multiple_of((qi + PAD_BLOCKS) * blk, blk), blk)
    for i in heads:
        kblk = k_ref[0, :, hcols(i)]
        kaug[i, own_rows, 0:HEAD_DIM] = kblk.astype(BF16)
        kaug[i, own_rows, HEAD_DIM:AUG] = hot
        vt[i, qi + PAD_BLOCKS] = v_ref[0, :, hcols(i)].T.astype(BF16)
        kmean[i] = jnp.where(row_n == qi, jnp.mean(kblk, axis=0, keepdims=True), kmean[i])

    qf = qi.astype(F32)

    gate = [lax.dot_general(kmean[i], q_ref[0, :, hcols(i)], NT_DIMS,
                            precision=lax.Precision.HIGHEST, preferred_element_type=F32)
            for i in heads]
    for i in heads:
        qaug[i, 0:HEAD_DIM, :] = (
            q_ref[0, :, hcols(i)] * (HEAD_DIM ** -0.5 * LOG2E)).T.astype(BF16)
    n_io = lax.broadcasted_iota(I32, gate[0].shape, 0).astype(F32)
    g = [jnp.where(n_io < qf, gate[i], -jnp.inf) for i in heads]
    sel = [jnp.zeros(gate[0].shape, F32) for i in heads]
    for _ in range(MOBA_TOPK):
        mx = [jnp.max(g[i], axis=0, keepdims=True) for i in heads]
        idx = [jnp.min(jnp.where(g[i] == mx[i], n_io, float(nb)), axis=0, keepdims=True)
               for i in heads]
        hit = [n_io == idx[i] for i in heads]
        sel = [jnp.where(hit[i], 1.0, sel[i]) for i in heads]
        g = [jnp.where(hit[i], -jnp.inf, g[i]) for i in heads]
    for i in heads:
        far = jnp.concatenate([far_ref[i]] * (blk // LANES), axis=1)
        picked = jnp.where(sel[i] > 0.0, jnp.where(n_io < qf - 1.0, far, 0.0), NEG)
        bias = jnp.where(n_io < qf, picked, jnp.where(n_io == qf, 0.0, NEG))
        bias_hi = bias.astype(BF16)
        qaug[i, HEAD_DIM:HEAD_DIM + nb, :] = bias_hi
        qaug[i, HEAD_DIM + NB_MAX:HEAD_DIM + NB_MAX + nb, :] = (
            bias - bias_hi.astype(F32)).astype(BF16)

    def scores(i, padded_block):
        r0 = pl.multiple_of(padded_block * blk, blk)
        return jnp.dot(kaug[i, pl.ds(r0, blk), :], qaug[i], preferred_element_type=F32)

    for i in heads:
        s_scr[i] = scores(i, qi + PAD_BLOCKS) + own_ref[i]
        acc_s[i] = jnp.zeros((HEAD_DIM, blk), F32)

    carry0 = []
    for i in heads:
        s_next = scores(i, qi - 1 + PAD_BLOCKS) + adj_ref[i]
        s_cur = s_scr[i]
        m0 = jnp.max(s_cur, axis=0, keepdims=True)
        p = jnp.exp2(s_cur - m0)
        p_scr[i] = p.astype(BF16)
        s_scr[i] = s_next
        carry0 += [m0, jnp.sum(p, axis=0, keepdims=True), jnp.ones_like(m0)]

    def step(t, carry):
        n = carry[-1]
        reduce_block(n)
        out = []
        for i in heads:
            m_prev, l_prev, alpha_prev = carry[3 * i:3 * i + 3]
            s_next = scores(i, qi - t - 1 + PAD_BLOCKS)
            pv = jnp.dot(vt[i, qi - t + 1 + PAD_BLOCKS], p_scr[i], preferred_element_type=F32)
            acc_s[i] = alpha_prev * acc_s[i] + pv
            s_cur = s_scr[i]
            m_new = jnp.maximum(m_prev, jnp.max(s_cur, axis=0, keepdims=True))
            alpha = jnp.exp2(m_prev - m_new)
            p = jnp.exp2(s_cur - m_new)
            p_scr[i] = p.astype(BF16)
            s_scr[i] = s_next
            out += [m_new, alpha * l_prev + jnp.sum(p, axis=0, keepdims=True), alpha]
        advance_stream(n)
        return tuple(out) + (n + 1,)

    fin = lax.fori_loop(1, qi + 2, step, tuple(carry0) + (count[0],))
    for i in heads:
        o_ref[0, :, hcols(i)] = (acc_s[i] / fin[3 * i + 1]).T.astype(o_ref.dtype)
    count[0] = fin[-1]

    @pl.when(grid_pos == n_steps - 1)
    def _():
        def drain(n, carry):
            reduce_block(n)
            advance_stream(n)
            return carry

        lax.fori_loop(fin[-1], n_mean, drain, 0)
        group_copy(n_mean // MEAN_GROUP - 1).wait()


def _moba_prompt(q, k, v, own, adj, far, cache_k, page_table, layer):
    b, s, d = q.shape
    nb = s // MOBA_BLOCK
    assert nb <= NB_MAX and FLAG_COL < AUG - HEAD_DIM and N_HEADS % HEADS_PER_STEP == 0
    blk = MOBA_BLOCK
    hp = HEADS_PER_STEP
    ppb = MOBA_BLOCK // PAGE_SIZE
    db, n_pages = page_table.shape
    n_mean = db * (n_pages // ppb)
    assert n_pages % ppb == 0 and n_mean % MEAN_GROUP == 0 and n_mean >= MEAN_DEPTH
    qspec = pl.BlockSpec((1, blk, hp * HEAD_DIM), lambda bi, h, qi, pt: (bi, qi, h))
    kvspec = qspec
    tile = pl.BlockSpec((hp, blk, blk), lambda bi, h, qi, pt: (h, 0, 0))
    anyspec = pl.BlockSpec(memory_space=pl.ANY)
    out, means = pl.pallas_call(
        functools.partial(_moba_prompt_body, layer),
        out_shape=[jax.ShapeDtypeStruct((b, s, d), BF16),
                   jax.ShapeDtypeStruct((n_mean, N_HEADS, HEAD_DIM), F32)],
        grid_spec=pltpu.PrefetchScalarGridSpec(
            num_scalar_prefetch=1,
            grid=(b, N_HEADS // hp, nb),
            in_specs=[qspec, kvspec, kvspec, tile, tile,
                      pl.BlockSpec((hp, 1, LANES), lambda bi, h, qi, pt: (h, 0, 0)), anyspec],
            out_specs=[qspec, anyspec],
            scratch_shapes=[pltpu.VMEM((hp, (nb + PAD_BLOCKS) * blk, AUG), BF16),
                            pltpu.VMEM((hp, nb + PAD_BLOCKS, HEAD_DIM, blk), BF16),
                            pltpu.VMEM((hp, nb, HEAD_DIM), F32),
                            pltpu.VMEM((hp, AUG, blk), BF16),
                            pltpu.VMEM((hp, blk, blk), F32),
                            pltpu.VMEM((hp, blk, blk), BF16),
                            pltpu.VMEM((hp, HEAD_DIM, blk), F32),
                            pltpu.VMEM((MEAN_DEPTH, ppb, PAGE_SIZE, N_HEADS, HEAD_DIM), F32),
                            pltpu.VMEM((3, MEAN_GROUP, N_HEADS, HEAD_DIM), F32),
                            pltpu.SMEM((1,), I32),
                            pltpu.SemaphoreType.DMA((MEAN_DEPTH,)),
                            pltpu.SemaphoreType.DMA((2,))]),
        compiler_params=_params(("arbitrary", "arbitrary", "arbitrary")),
        name="moba_prompt",
    )(page_table, q, k, v, own, adj, far, cache_k)
    return out, means.reshape(db, n_pages // ppb, N_HEADS, HEAD_DIM)


def _sample_select_body(q_ref, bm_ref, sel_ref):
    nblk = bm_ref.shape[1] // N_HEADS
    q = q_ref[0]
    for h in range(N_HEADS):
        qh = q[:, h * HEAD_DIM:(h + 1) * HEAD_DIM]
        bmh = bm_ref[0, pl.ds(h, nblk, stride=N_HEADS), :]
        g = lax.dot_general(qh, bmh, NT_DIMS, precision=lax.Precision.HIGHEST,
                            preferred_element_type=F32)
        n_iota = lax.broadcasted_iota(I32, g.shape, 1).astype(F32)
        lane = lax.broadcasted_iota(I32, (g.shape[0], LANES), 1)
        out = jnp.zeros((g.shape[0], LANES), I32)
        for r in range(MOBA_TOPK):
            mx = jnp.max(g, axis=1, keepdims=True)
            idx = jnp.min(jnp.where(g == mx, n_iota, float(nblk)), axis=1, keepdims=True)
            out = jnp.where(lane == r, idx.astype(I32), out)
            g = jnp.where(n_iota == idx, -jnp.inf, g)
        sel_ref[0, h] = out


def _sample_select(q3, blk_mean):
    db, t, d = q3.shape
    nblk = blk_mean.shape[1]
    bm2 = blk_mean.reshape(db, nblk * N_HEADS, HEAD_DIM)
    return pl.pallas_call(
        _sample_select_body,
        out_shape=jax.ShapeDtypeStruct((db, N_HEADS, t, LANES), I32),
        grid=(db,),
        in_specs=[pl.BlockSpec((1, t, d), lambda b: (b, 0, 0)),
                  pl.BlockSpec((1, nblk * N_HEADS, HEAD_DIM), lambda b: (b, 0, 0))],
        out_specs=pl.BlockSpec((1, N_HEADS, t, LANES), lambda b: (b, 0, 0, 0)),
        compiler_params=_params(("arbitrary",)),
        name="sample_select",
    )(q3, bm2)


GATHER_SLOTS = 4


def _sample_attn_body(layer, past_len, pt_ref, sel_ref, q_ref, kn_ref, vn_ref, far_ref, last_ref,
                      rb_ref, ck_hbm, cv_hbm, o_ref, kbuf, vbuf, sem):
    bi = pl.program_id(0)
    h = pl.program_id(1)
    nh = pl.num_programs(1)
    t_len = q_ref.shape[1]
    ppb = MOBA_BLOCK // PAGE_SIZE
    n_sel = t_len * MOBA_TOPK
    step = bi * nh + h
    n_steps = pl.num_programs(0) * nh
    slot = step % GATHER_SLOTS

    def sel_at(s_idx, j):
        return sel_ref[s_idx * n_sel + j]

    def copies(s_idx):
        b_idx = s_idx // nh
        h_idx = s_idx - b_idx * nh
        sl = s_idx % GATHER_SLOTS
        out = []
        for j in range(n_sel):
            blk_id = sel_at(s_idx, j)
            for pg in range(ppb):
                page = pt_ref[b_idx, blk_id * ppb + pg]
                dst = pl.ds(j * MOBA_BLOCK + pg * PAGE_SIZE, PAGE_SIZE)
                out.append(pltpu.make_async_copy(
                    ck_hbm.at[layer, page, :, h_idx, :], kbuf.at[sl, dst, :], sem.at[0, sl]))
                out.append(pltpu.make_async_copy(
                    cv_hbm.at[layer, page, :, h_idx, :], vbuf.at[sl, dst, :], sem.at[1, sl]))
        return out

    @pl.when(step == 0)
    def _():
        for ahead in range(GATHER_SLOTS - 1):
            for cp in copies(ahead):
                cp.start()

    @pl.when(step + GATHER_SLOTS - 1 < n_steps)
    def _():
        for cp in copies(step + GATHER_SLOTS - 1):
            cp.start()

    def rb_get(k):
        return rb_ref[h * N_BUCKETS + k] * LOG2E

    q = q_ref[0] * (HEAD_DIM ** -0.5 * LOG2E)
    kn = kn_ref[0]
    vn = vn_ref[0]
    n_cols = n_sel * MOBA_BLOCK
    col = lax.broadcasted_iota(I32, (t_len, n_cols), 1)
    row = lax.broadcasted_iota(I32, (t_len, n_cols), 0)
    far = jnp.broadcast_to(jnp.concatenate([far_ref[0]] * (MOBA_BLOCK // LANES), axis=1),
                           (t_len, MOBA_BLOCK))
    last = last_ref[0, 0:t_len, :]
    last_block = past_len // MOBA_BLOCK - 1
    bias = jnp.concatenate(
        [jnp.where(sel_at(step, j) == last_block, last, far) for j in range(n_sel)], axis=1)
    per_q = MOBA_TOPK * MOBA_BLOCK
    mine = (col >= row * per_q) & (col < (row + 1) * per_q)
    bias = jnp.where(mine, bias, NEG)

    row1 = lax.broadcasted_iota(I32, (t_len, LANES), 0)
    s_own = []
    for j in range(t_len):
        d_o = row1 - j
        s_j = jnp.sum(q * kn[j:j + 1, :], axis=1, keepdims=True)
        s_own.append(jnp.where(d_o >= 0, s_j + _t5_bias(d_o, rb_get), NEG)[:, 0:1])

    for cp in copies(step):
        cp.wait()
    s = lax.dot_general(q.astype(BF16), kbuf[slot].astype(BF16), NT_DIMS,
                        preferred_element_type=F32) + bias
    m = jnp.max(s, axis=1, keepdims=True)
    for s_j in s_own:
        m = jnp.maximum(m, s_j)
    p = jnp.exp2(s - m)
    den = jnp.sum(p, axis=1, keepdims=True)
    acc = jnp.dot(p.astype(BF16), vbuf[slot].astype(BF16), preferred_element_type=F32)
    for j, s_j in enumerate(s_own):
        p_j = jnp.exp2(s_j - m)
        den = den + p_j
        acc = acc + p_j * vn[j:j + 1, :]
    o_ref[0] = acc / den


def _sample_attn(q3, k3, v3, far, last, cache_k, cache_v, page_table, sel_flat, rb_flat, layer):
    db, t, d = q3.shape
    past_len = page_table.shape[1] * PAGE_SIZE
    assert past_len % MOBA_BLOCK == 0 and t <= SUBLANES
    n_cols = t * MOBA_TOPK * MOBA_BLOCK
    spec = pl.BlockSpec((1, t, HEAD_DIM), lambda b, h, pt, sl: (b, 0, h))
    farspec = pl.BlockSpec((1, 1, LANES), lambda b, h, pt, sl: (h, 0, 0))
    lastspec = pl.BlockSpec((1, SUBLANES, MOBA_BLOCK), lambda b, h, pt, sl: (h, 0, 0))
    anyspec = pl.BlockSpec(memory_space=pl.ANY)
    return pl.pallas_call(
        functools.partial(_sample_attn_body, layer, past_len),
        out_shape=jax.ShapeDtypeStruct((db, t, d), F32),
        grid_spec=pltpu.PrefetchScalarGridSpec(
            num_scalar_prefetch=2,
            grid=(db, N_HEADS),
            in_specs=[spec, spec, spec, farspec, lastspec,
                      pl.BlockSpec(memory_space=pltpu.SMEM), anyspec, anyspec],
            out_specs=spec,
            scratch_shapes=[pltpu.VMEM((GATHER_SLOTS, n_cols, HEAD_DIM), F32),
                            pltpu.VMEM((GATHER_SLOTS, n_cols, HEAD_DIM), F32),
                            pltpu.SemaphoreType.DMA((2, GATHER_SLOTS))]),
        compiler_params=_params(("arbitrary", "arbitrary")),
        name="sample_attn",
    )(page_table, sel_flat, q3, k3, v3, far, last, rb_flat, cache_k, cache_v)


SUB = 8


def _gla_body(q_ref, k_ref, v_ref, lf_ref, gs_ref, gn_ref, s0_ref, o_ref, sfin_ref,
              st_scr, b_scr):
    ci = pl.program_id(1)
    chunk = q_ref.shape[0]
    nsub = chunk // SUB

    @pl.when(ci == 0)
    def _():
        for h in range(HG_HEADS):
            st_scr[h] = s0_ref[0, h].T

    r_i = lax.broadcasted_iota(I32, (chunk, chunk), 0)
    c_i = lax.broadcasted_iota(I32, (chunk, chunk), 1)
    tri = jnp.where(r_i >= c_i, 1.0, 0.0).astype(F32)
    b_all = jnp.dot(tri, lf_ref[...], precision=lax.Precision.HIGHEST,
                    preferred_element_type=F32) * LOG2E
    b_scr[...] = b_all
    neg_inf = -jnp.inf
    t_sub = lax.broadcasted_iota(I32, (SUB, HG_DIM), 0)
    causal = [t_sub >= s for s in range(SUB)]

    def zeros(n):
        return [jnp.zeros((n, HG_DIM), F32)] if n else []

    def cols(h):
        return slice(h * HG_DIM, (h + 1) * HG_DIM)


    o_state, att = [], []
    for h in range(HG_HEADS):
        sl = cols(h)
        b, q, k = b_all[:, sl], q_ref[:, sl], k_ref[:, sl]
        o_state.append(lax.dot_general((q * jnp.exp2(b)).astype(BF16),
                                       st_scr[h].astype(BF16), NT_DIMS,
                                       preferred_element_type=F32))
        if nsub > 1:
            qs, ks = [], []
            for j in range(nsub - 1):
                lo, hi = SUB * j, SUB * (j + 1)
                e_j = b[hi - 1:hi, :]
                qs.append(jnp.concatenate(
                    zeros(hi) + [q[hi:] * jnp.exp2(b[hi:] - e_j)], axis=0))
                ks.append(jnp.concatenate(
                    zeros(lo) + [k[lo:hi] * jnp.exp2(e_j - b[lo:hi])] + zeros(chunk - hi), axis=0))
            att.append(lax.dot_general(jnp.concatenate(qs, axis=1).astype(BF16),
                                       jnp.concatenate(ks, axis=1).astype(BF16), NT_DIMS,
                                       preferred_element_type=F32))

    for h in range(HG_HEADS):
        sl = cols(h)
        b = b_all[:, sl]
        b_last = b[chunk - 1:chunk, :]
        ke = k_ref[:, sl] * jnp.exp2(b_last - b)
        st_scr[h] = st_scr[h] * jnp.exp2(b_last) + lax.dot_general(
            v_ref[:, sl].astype(BF16), ke.astype(BF16), TN_DIMS, preferred_element_type=F32)

    o_diag = []
    for h in range(HG_HEADS):
        sl = cols(h)
        diag = []
        for i in range(nsub):
            r0 = SUB * i
            b_i, q_i = b_all[r0:r0 + SUB, sl], q_ref[r0:r0 + SUB, sl]
            o_i = jnp.zeros((SUB, HG_DIM), F32)
            for s in range(SUB):
                r = r0 + s
                dec = jnp.exp2(jnp.where(causal[s], b_i - b_scr[r:r + 1, sl], neg_inf))
                a = jnp.sum(dec * q_i * k_ref[r:r + 1, sl], axis=1, keepdims=True)
                o_i = o_i + a * v_ref[r:r + 1, sl]
            diag.append(o_i)
        o_diag.append(jnp.concatenate(diag, axis=0) if nsub > 1 else diag[0])

    for h in range(HG_HEADS):
        sl = cols(h)
        o = o_state[h] + o_diag[h]
        if nsub > 1:
            o = o + jnp.dot(att[h].astype(BF16), v_ref[:, sl].astype(BF16),
                            preferred_element_type=F32)
        o = o * lax.rsqrt(jnp.mean(o * o, axis=-1, keepdims=True) + EPS)
        o_ref[:, sl] = (o * gn_ref[:, sl] * gs_ref[:, sl]).astype(o_ref.dtype)

    @pl.when(ci == pl.num_programs(1) - 1)
    def _():
        for h in range(HG_HEADS):
            sfin_ref[0, h] = st_scr[h].T


def _gla(q, k, v, lf, gs, g_norm, s0, chunk):
    m, d = q.shape
    nb = s0.shape[0]
    nc = m // nb // chunk
    row = pl.BlockSpec((chunk, d), lambda b, c: (b * nc + c, 0))
    state = pl.BlockSpec((1, HG_HEADS, HG_DIM, HG_DIM), lambda b, c: (b, 0, 0, 0))
    return pl.pallas_call(
        _gla_body,
        out_shape=[jax.ShapeDtypeStruct((m, d), BF16),
                   jax.ShapeDtypeStruct(s0.shape, F32)],
        grid=(nb, nc),
        in_specs=[row, row, row, row, row, pl.BlockSpec((1, d), lambda b, c: (0, 0)), state],
        out_specs=[row, state],
        scratch_shapes=[pltpu.VMEM((HG_HEADS, HG_DIM, HG_DIM), F32),
                        pltpu.VMEM((chunk, d), F32)],
        compiler_params=_params(("arbitrary", "arbitrary")),
        name="gla",
    )(q, k, v, lf, gs, g_norm.reshape(1, d), s0)


PROMPT_TM = 512
MLP_TM = 1024
MLP_TF = 1024
GLA_CHUNK = 64
GLA_SAMPLE_CHUNK = 16


def kernel(x_prompt, x_sample, cache_k, cache_v, state_hgrn, page_table, norm_mix, norm_ffn,
           norm_final, attn_w_qkv, attn_w_o, rel_bias, hg_w_in, hg_lb_param, hg_norm, hg_w_o,
           ffn_w1, ffn_w2):
    b, s, d = x_prompt.shape
    db, t, _ = x_sample.shape
    mp, ms = b * s, db * t
    xp = x_prompt.reshape(mp, d)
    xs = x_sample.reshape(ms, d)
    rb_flat = rel_bias.T.reshape(-1)
    w_qkv = attn_w_qkv.astype(BF16)
    w_ao = attn_w_o.astype(BF16)
    w_in = hg_w_in.astype(BF16)
    w_ho = hg_w_o.astype(BF16)
    w1 = ffn_w1.astype(BF16)
    w2 = ffn_w2.astype(BF16)

    own, adj, far, last = _bias_tables(rb_flat)
    qp, kp, vp = _qkv_proj(xp, norm_mix[0], w_qkv[0], PROMPT_TM)
    op, blk_mean = _moba_prompt(qp.reshape(b, s, d), kp.reshape(b, s, d), vp.reshape(b, s, d),
                                own, adj, far, cache_k, page_table, 0)
    xp = _proj_mlp(xp, op.reshape(mp, d), w_ao[0], norm_ffn[0], w1[0], w2[0], norm_final,
                   False, MLP_TM, MLP_TF)

    qs, ks, vs = _qkv_proj(xs, norm_mix[0], w_qkv[0], ms)
    q3 = qs.reshape(db, t, d)
    sel = _sample_select(q3, blk_mean)
    sel_flat = sel[:, :, :, :MOBA_TOPK].reshape(-1)
    os_ = _sample_attn(q3, ks.reshape(db, t, d), vs.reshape(db, t, d), far, last,
                       cache_k, cache_v, page_table, sel_flat, rb_flat, 0)
    xs = _proj_mlp(xs, os_.reshape(ms, d).astype(BF16), w_ao[0], norm_ffn[0], w1[0], w2[0],
                   norm_final, False, ms, MLP_TF)

    hq, hk, hv, hlf, hgs = _hgrn_proj(xp, norm_mix[1], w_in[0], hg_lb_param, 1, PROMPT_TM)
    s0p = jnp.zeros((b, HG_HEADS, HG_DIM, HG_DIM), F32)
    hop, s_prompt = _gla(hq, hk, hv, hlf, hgs, hg_norm[0], s0p, GLA_CHUNK)
    y_prompt = _proj_mlp(xp, hop, w_ho[0], norm_ffn[1], w1[1], w2[1], norm_final,
                         True, MLP_TM, MLP_TF)

    outs = _hgrn_proj(xs, norm_mix[1], w_in[0], hg_lb_param, 1, ms)
    pad = GLA_SAMPLE_CHUNK - t
    padded = [jnp.pad(a.reshape(db, t, d), ((0, 0), (0, pad), (0, 0))).reshape(
        db * GLA_SAMPLE_CHUNK, d) for a in outs]
    hos, s_sample = _gla(*padded, hg_norm[0], state_hgrn[0], GLA_SAMPLE_CHUNK)
    hos = hos.reshape(db, GLA_SAMPLE_CHUNK, d)[:, :t].reshape(ms, d)
    y_sample = _proj_mlp(xs, hos, w_ho[0], norm_ffn[1], w1[1], w2[1], norm_final,
                         True, ms, MLP_TF)

    hd = (N_HEADS, HEAD_DIM)
    return (y_prompt.reshape(b, s, d), y_sample.reshape(db, t, d),
            kp.reshape(1, b, s, *hd), vp.reshape(1, b, s, *hd),
            ks.reshape(1, db, t, *hd), vs.reshape(1, db, t, *hd),
            s_prompt[None], s_sample[None])
```

```python
import functools
import math

import jax
import jax.numpy as jnp
from jax import lax
from jax.experimental import pallas as pl
from jax.experimental.pallas import tpu as pltpu

F32 = jnp.float32
BF16 = jnp.bfloat16
I32 = jnp.int32

N_HEADS = 8
HEAD_DIM = 128
MOBA_BLOCK = 256
MOBA_TOPK = 3
PAGE_SIZE = 128
N_BUCKETS = 32
MAX_DISTANCE = 128
MAX_EXACT = N_BUCKETS // 2
HG_HEADS = 8
HG_DIM = 128
EPS = 1e-6

LANES = 128
SUBLANES = 8
VMEM_LIMIT = 56 * 1024 * 1024

NEG = -1e30
LOG2E = math.log2(math.e)

NT_DIMS = (((1,), (1,)), ((), ()))
TN_DIMS = (((0,), (0,)), ((), ()))


def _params(sem):
    return pltpu.CompilerParams(dimension_semantics=sem, vmem_limit_bytes=VMEM_LIMIT)


def _rms(x, g):
    ms = jnp.mean(x * x, axis=-1, keepdims=True)
    return x * lax.rsqrt(ms + EPS) * g


def _t5_bias(dist, rb_get):
    n = jnp.maximum(dist, 0)
    nf = jnp.maximum(n, 1).astype(F32)
    large = MAX_EXACT + (jnp.log(nf / MAX_EXACT) / math.log(MAX_DISTANCE / MAX_EXACT)
                         * (N_BUCKETS - MAX_EXACT)).astype(I32)
    large = jnp.minimum(large, N_BUCKETS - 1)
    bucket = jnp.where(n < MAX_EXACT, n, large)
    out = jnp.zeros(dist.shape, F32)
    for k in range(N_BUCKETS):
        out = jnp.where(bucket == k, rb_get(k), out)
    return out


CAST_BLOCK_BYTES = 4 * 1024 * 1024


def _cast_body(x_ref, o_ref):
    o_ref[...] = x_ref[...].astype(o_ref.dtype)


def _to_bf16(w):
    cols = w.shape[-1]
    rows = math.prod(w.shape[:-1])
    budget_rows = max(2 * SUBLANES, CAST_BLOCK_BYTES // (cols * 4))
    br = min(rows, 1 << (budget_rows.bit_length() - 1))
    assert rows % br == 0
    spec = pl.BlockSpec((br, cols), lambda i: (i, 0))
    out = pl.pallas_call(
        _cast_body,
        out_shape=jax.ShapeDtypeStruct((rows, cols), BF16),
        grid=(rows // br,),
        in_specs=[spec],
        out_specs=spec,
        compiler_params=_params(("arbitrary",)),
        name="to_bf16",
    )(w.reshape(rows, cols))
    return out.reshape(w.shape)


def _qkv_body(x_ref, g_ref, w_ref, q_ref, k_ref, v_ref):
    d = x_ref.shape[1]
    xn = _rms(x_ref[...], g_ref[...]).astype(BF16)
    for c, o_ref in enumerate((q_ref, k_ref, v_ref)):
        o_ref[...] = jnp.dot(xn, w_ref[:, c * d:(c + 1) * d], preferred_element_type=F32)


def _qkv_proj(x, g, w_bf16, tm):
    m, d = x.shape
    row = pl.BlockSpec((tm, d), lambda i: (i, 0))
    return pl.pallas_call(
        _qkv_body,
        out_shape=[jax.ShapeDtypeStruct((m, d), F32)] * 3,
        grid=(m // tm,),
        in_specs=[row, pl.BlockSpec((1, d), lambda i: (0, 0)),
                  pl.BlockSpec((d, 3 * d), lambda i: (0, 0))],
        out_specs=[row] * 3,
        compiler_params=_params(("arbitrary",)),
        name="qkv_proj",
    )(x, g.reshape(1, d), w_bf16)


def _hgrn_proj_body(layer, seq_rows, x_ref, g_ref, w_ref, lbp_ref,
                    q_ref, k_ref, v_ref, lf_ref, gs_ref):
    d = x_ref.shape[1]
    xn = _rms(x_ref[...], g_ref[...]).astype(BF16)

    def proj(c):
        return jnp.dot(xn, w_ref[:, c * d:(c + 1) * d], preferred_element_type=F32)

    lbp = lbp_ref[...]
    e = jnp.exp(lbp - jnp.max(lbp, axis=0, keepdims=True))
    sm = e / jnp.sum(e, axis=0, keepdims=True)
    cum = sm[0:1]
    for r in range(1, layer + 1):
        cum = cum + sm[r:r + 1]
    lb = cum - sm[0:1]

    q_pre = proj(0)
    q_ref[...] = q_pre * jax.nn.sigmoid(q_pre) * (HG_DIM ** -0.5)
    fgate = lb + (1.0 - lb) * jax.nn.sigmoid(proj(1))
    if seq_rows is not None:
        row = lax.broadcasted_iota(I32, fgate.shape, 0)
        fgate = jnp.where((row & (seq_rows[0] - 1)) < seq_rows[1], fgate, 1.0)
    lf_ref[...] = jnp.log(fgate)
    k_ref[...] = 1.0 - fgate
    v_ref[...] = proj(2)
    g_pre = proj(3)
    gs_ref[...] = g_pre * jax.nn.sigmoid(g_pre)


def _hgrn_proj(x, g, w_bf16, lb_param, layer, tm, seq_rows=None):
    m, d = x.shape
    depth = lb_param.shape[0]
    assert seq_rows is None or (tm % seq_rows[0] == 0 and seq_rows[0] & (seq_rows[0] - 1) == 0)
    row = pl.BlockSpec((tm, d), lambda i: (i, 0))
    return pl.pallas_call(
        functools.partial(_hgrn_proj_body, layer, seq_rows),
        out_shape=[jax.ShapeDtypeStruct((m, d), F32)] * 5,
        grid=(m // tm,),
        in_specs=[row, pl.BlockSpec((1, d), lambda i: (0, 0)),
                  pl.BlockSpec((d, 4 * d), lambda i: (0, 0)),
                  pl.BlockSpec((depth, d), lambda i: (0, 0))],
        out_specs=[row] * 5,
        compiler_params=_params(("arbitrary",)),
        name="hgrn_proj",
    )(x, g.reshape(1, d), w_bf16, lb_param)


def _proj_mlp_body(final_norm, x_ref, o_ref, wo_ref, g_ref, w1_ref, w2_ref, gf_ref,
                   out_ref, x1_scr, xn_scr):
    f = pl.program_id(1)

    @pl.when(f == 0)
    def _():
        x1 = x_ref[...] + jnp.dot(o_ref[...], wo_ref[...], preferred_element_type=F32)
        x1_scr[...] = x1
        xn_scr[...] = _rms(x1, g_ref[...]).astype(BF16)
        out_ref[...] = jnp.zeros_like(out_ref)

    h = jnp.maximum(jnp.dot(xn_scr[...], w1_ref[...], preferred_element_type=F32), 0.0)
    out_ref[...] += jnp.dot((h * h).astype(BF16), w2_ref[...], preferred_element_type=F32)

    @pl.when(f == pl.num_programs(1) - 1)
    def _():
        y = x1_scr[...] + out_ref[...]
        if final_norm:
            y = _rms(y, gf_ref[...])
        out_ref[...] = y


def _proj_mlp(x, o_bf16, wo, g_ffn, w1, w2, g_final, final_norm, tm, tf):
    m, d = x.shape
    dff = w1.shape[1]
    row = pl.BlockSpec((tm, d), lambda i, f: (i, 0))
    vec = pl.BlockSpec((1, d), lambda i, f: (0, 0))
    return pl.pallas_call(
        functools.partial(_proj_mlp_body, final_norm),
        out_shape=jax.ShapeDtypeStruct((m, d), F32),
        grid=(m // tm, dff // tf),
        in_specs=[row, row, pl.BlockSpec((d, d), lambda i, f: (0, 0)), vec,
                  pl.BlockSpec((d, tf), lambda i, f: (0, f)),
                  pl.BlockSpec((tf, d), lambda i, f: (f, 0)), vec],
        out_specs=row,
        scratch_shapes=[pltpu.VMEM((tm, d), F32), pltpu.VMEM((tm, d), BF16)],
        compiler_params=_params(("arbitrary", "arbitrary")),
        name="proj_mlp",
    )(x, o_bf16, wo, g_ffn.reshape(1, d), w1, w2, g_final.reshape(1, d))


def _bias_tables_body(rb_ref, own_ref, adj_ref, far_ref, last_ref):
    h = pl.program_id(0)
    shape = (MOBA_BLOCK, MOBA_BLOCK)
    d_own = lax.broadcasted_iota(I32, shape, 1) - lax.broadcasted_iota(I32, shape, 0)

    def rb_get(k):
        return rb_ref[h * N_BUCKETS + k] * LOG2E

    own_ref[0] = jnp.where(d_own >= 0, _t5_bias(d_own, rb_get), NEG)
    adj_ref[0] = _t5_bias(d_own + MOBA_BLOCK, rb_get)
    d_far = jnp.full((1, LANES), MOBA_BLOCK + 1, I32)
    far_ref[0] = _t5_bias(d_far, rb_get)
    lshape = (SUBLANES, MOBA_BLOCK)
    d_last = (MOBA_BLOCK + lax.broadcasted_iota(I32, lshape, 0)
              - lax.broadcasted_iota(I32, lshape, 1))
    last_ref[0] = _t5_bias(d_last, rb_get)


def _bias_tables(rb_flat):
    tile = pl.BlockSpec((1, MOBA_BLOCK, MOBA_BLOCK), lambda h: (h, 0, 0))
    return pl.pallas_call(
        _bias_tables_body,
        out_shape=[jax.ShapeDtypeStruct((N_HEADS, MOBA_BLOCK, MOBA_BLOCK), F32),
                   jax.ShapeDtypeStruct((N_HEADS, MOBA_BLOCK, MOBA_BLOCK), F32),
                   jax.ShapeDtypeStruct((N_HEADS, 1, LANES), F32),
                   jax.ShapeDtypeStruct((N_HEADS, SUBLANES, MOBA_BLOCK), F32)],
        grid=(N_HEADS,),
        in_specs=[pl.BlockSpec(memory_space=pltpu.SMEM)],
        out_specs=[tile, tile, pl.BlockSpec((1, 1, LANES), lambda h: (h, 0, 0)),
                   pl.BlockSpec((1, SUBLANES, MOBA_BLOCK), lambda h: (h, 0, 0))],
        compiler_params=_params(("arbitrary",)),
        name="bias_tables",
    )(rb_flat)


PAD_BLOCKS = 1
NB_MAX = 32
AUG = 2 * HEAD_DIM
FLAG_COL = 2 * NB_MAX
HEADS_PER_STEP = 4


MEAN_DEPTH = 8
MEAN_GROUP = 16


def _moba_prompt_body(layer, pt_ref, q_ref, k_ref, v_ref, own_ref, adj_ref, far_ref, ck_hbm,
                      o_ref, bm_hbm, kaug, vt, kmean, qaug, s_scr, p_scr, acc_s,
                      ring, stage, count, sem_in, sem_out):
    qi = pl.program_id(2)
    nb = kmean.shape[1]
    blk = MOBA_BLOCK
    heads = range(HEADS_PER_STEP)
    ppb = MOBA_BLOCK // PAGE_SIZE
    blocks_per_seq = pt_ref.shape[1] // ppb
    n_mean = pt_ref.shape[0] * blocks_per_seq
    grid_pos = (pl.program_id(0) * pl.num_programs(1) + pl.program_id(1)) * pl.num_programs(2) + qi
    n_steps = pl.num_programs(0) * pl.num_programs(1) * pl.num_programs(2)

    def hcols(i):
        return slice(i * HEAD_DIM, (i + 1) * HEAD_DIM)

    def page_copies(n):
        seq = n // blocks_per_seq
        first_page = (n - seq * blocks_per_seq) * ppb
        slot = n % MEAN_DEPTH
        return [pltpu.make_async_copy(ck_hbm.at[layer, pt_ref[seq, first_page + pg]],
                                      ring.at[slot, pg], sem_in.at[slot]) for pg in range(ppb)]

    def group_copy(g):
        half = g % 2
        return pltpu.make_async_copy(
            stage.at[half], bm_hbm.at[pl.ds(pl.multiple_of(g * MEAN_GROUP, MEAN_GROUP), MEAN_GROUP)],
            sem_out.at[half])

    def reduce_block(n):
        slot = n % MEAN_DEPTH
        tot = jnp.sum(ring[slot, 0], axis=0)
        for pg in range(1, ppb):
            tot = tot + jnp.sum(ring[slot, pg], axis=0)
        g = n // MEAN_GROUP
        stage[jnp.where(n < n_mean, g % 2, 2), n - g * MEAN_GROUP] = tot * (1.0 / MOBA_BLOCK)

    def advance_stream(n):
        @pl.when(n + MEAN_DEPTH < n_mean)
        def _():
            for cp in page_copies(n + MEAN_DEPTH):
                cp.start()

        @pl.when(n + 1 < n_mean)
        def _():
            for cp in page_copies(n + 1):
                cp.wait()

        g = n // MEAN_GROUP

        @pl.when((n - g * MEAN_GROUP == MEAN_GROUP - 1) & (n < n_mean))
        def _():
            @pl.when(g >= 1)
            def _():
                group_copy(g - 1).wait()
            group_copy(g).start()

    @pl.when(grid_pos == 0)
    def _():
        count[0] = 0
        for n in range(MEAN_DEPTH):
            for cp in page_copies(n):
                cp.start()
        for cp in page_copies(0):
            cp.wait()

    @pl.when(qi == 0)
    def _():
        lane_p = lax.broadcasted_iota(I32, (PAD_BLOCKS * blk, AUG), 1)
        pad_keys = jnp.where(lane_p == HEAD_DIM + FLAG_COL, 1.0, 0.0).astype(BF16)
        row_q = lax.broadcasted_iota(I32, (AUG - HEAD_DIM, blk), 0)
        for i in heads:
            kaug[i, 0:PAD_BLOCKS * blk, :] = pad_keys
            for n in range(PAD_BLOCKS):
                vt[i, n] = jnp.zeros((HEAD_DIM, blk), BF16)
            kmean[i] = jnp.zeros((nb, HEAD_DIM), F32)
            qaug[i, HEAD_DIM:AUG, :] = jnp.where(row_q == FLAG_COL, NEG, 0.0).astype(BF16)

    lane = lax.broadcasted_iota(I32, (blk, LANES), 1)
    hot = jnp.where(lane == qi, 1.0, jnp.where(lane == NB_MAX + qi, 1.0, 0.0)).astype(BF16)
    row_n = lax.broadcasted_iota(I32, (nb, HEAD_DIM), 0)
    own_rows = pl.ds(pl.multiple_of((qi + PAD_BLOCKS) * blk, blk), blk)
    for i in heads:
        kblk = k_ref[0, :, hcols(i)]
        kaug[i, own_rows, 0:HEAD_DIM] = kblk.astype(BF16)
        kaug[i, own_rows, HEAD_DIM:AUG] = hot
        vt[i, qi + PAD_BLOCKS] = v_ref[0, :, hcols(i)].T.astype(BF16)
        kmean[i] = jnp.where(row_n == qi, jnp.mean(kblk, axis=0, keepdims=True), kmean[i])

    qf = qi.astype(F32)

    gate = [lax.dot_general(kmean[i], q_ref[0, :, hcols(i)], NT_DIMS,
                            precision=lax.Precision.HIGHEST, preferred_element_type=F32)
            for i in heads]
    for i in heads:
        qaug[i, 0:HEAD_DIM, :] = (
            q_ref[0, :, hcols(i)] * (HEAD_DIM ** -0.5 * LOG2E)).T.astype(BF16)
    n_io = lax.broadcasted_iota(I32, gate[0].shape, 0).astype(F32)
    g = [jnp.where(n_io < qf, gate[i], -jnp.inf) for i in heads]
    sel = [jnp.zeros(gate[0].shape, F32) for i in heads]
    for _ in range(MOBA_TOPK):
        mx = [jnp.max(g[i], axis=0, keepdims=True) for i in heads]
        idx = [jnp.min(jnp.where(g[i] == mx[i], n_io, float(nb)), axis=0, keepdims=True)
               for i in heads]
        hit = [n_io == idx[i] for i in heads]
        sel = [jnp.where(hit[i], 1.0, sel[i]) for i in heads]
        g = [jnp.where(hit[i], -jnp.inf, g[i]) for i in heads]
    for i in heads:
        far = jnp.concatenate([far_ref[i]] * (blk // LANES), axis=1)
        picked = jnp.where(sel[i] > 0.0, jnp.where(n_io < qf - 1.0, far, 0.0), NEG)
        bias = jnp.where(n_io < qf, picked, jnp.where(n_io == qf, 0.0, NEG))
        bias_hi = bias.astype(BF16)
        qaug[i, HEAD_DIM:HEAD_DIM + nb, :] = bias_hi
        qaug[i, HEAD_DIM + NB_MAX:HEAD_DIM + NB_MAX + nb, :] = (
            bias - bias_hi.astype(F32)).astype(BF16)

    def scores(i, padded_block):
        r0 = pl.multiple_of(padded_block * blk, blk)
        return jnp.dot(kaug[i, pl.ds(r0, blk), :], qaug[i], preferred_element_type=F32)

    for i in heads:
        s_scr[i] = scores(i, qi + PAD_BLOCKS) + own_ref[i]
        acc_s[i] = jnp.zeros((HEAD_DIM, blk), F32)

    carry0 = []
    for i in heads:
        s_next = scores(i, qi - 1 + PAD_BLOCKS) + adj_ref[i]
        s_cur = s_scr[i]
        m0 = jnp.max(s_cur, axis=0, keepdims=True)
        p = jnp.exp2(s_cur - m0)
        p_scr[i] = p.astype(BF16)
        s_scr[i] = s_next
        carry0 += [m0, jnp.sum(p, axis=0, keepdims=True), jnp.ones_like(m0)]

    def step(t, carry):
        n = carry[-1]
        reduce_block(n)
        out = []
        for i in heads:
            m_prev, l_prev, alpha_prev = carry[3 * i:3 * i + 3]
            s_next = scores(i, qi - t - 1 + PAD_BLOCKS)
            pv = jnp.dot(vt[i, qi - t + 1 + PAD_BLOCKS], p_scr[i], preferred_element_type=F32)
            acc_s[i] = alpha_prev * acc_s[i] + pv
            s_cur = s_scr[i]
            m_new = jnp.maximum(m_prev, jnp.max(s_cur, axis=0, keepdims=True))
            alpha = jnp.exp2(m_prev - m_new)
            p = jnp.exp2(s_cur - m_new)
            p_scr[i] = p.astype(BF16)
            s_scr[i] = s_next
            out += [m_new, alpha * l_prev + jnp.sum(p, axis=0, keepdims=True), alpha]
        advance_stream(n)
        return tuple(out) + (n + 1,)

    fin = lax.fori_loop(1, qi + 1, step, tuple(carry0) + (count[0],))
    for i in heads:
        pv = jnp.dot(vt[i, PAD_BLOCKS], p_scr[i], preferred_element_type=F32)
        acc = fin[3 * i + 2] * acc_s[i] + pv
        o_ref[0, :, hcols(i)] = (acc / fin[3 * i + 1]).T.astype(o_ref.dtype)
    count[0] = fin[-1]

    @pl.when(grid_pos == n_steps - 1)
    def _():
        def drain(n, carry):
            reduce_block(n)
            advance_stream(n)
            return carry

        lax.fori_loop(fin[-1], n_mean, drain, 0)
        group_copy(n_mean // MEAN_GROUP - 1).wait()


def _moba_prompt(q, k, v, own, adj, far, cache_k, page_table, layer):
    b, s, d = q.shape
    nb = s // MOBA_BLOCK
    assert nb <= NB_MAX and FLAG_COL < AUG - HEAD_DIM and N_HEADS % HEADS_PER_STEP == 0
    blk = MOBA_BLOCK
    hp = HEADS_PER_STEP
    ppb = MOBA_BLOCK // PAGE_SIZE
    db, n_pages = page_table.shape
    n_mean = db * (n_pages // ppb)
    assert n_pages % ppb == 0 and n_mean % MEAN_GROUP == 0 and n_mean >= MEAN_DEPTH
    qspec = pl.BlockSpec((1, blk, hp * HEAD_DIM), lambda bi, h, qi, pt: (bi, qi, h))
    kvspec = qspec
    tile = pl.BlockSpec((hp, blk, blk), lambda bi, h, qi, pt: (h, 0, 0))
    anyspec = pl.BlockSpec(memory_space=pl.ANY)
    out, means = pl.pallas_call(
        functools.partial(_moba_prompt_body, layer),
        out_shape=[jax.ShapeDtypeStruct((b, s, d), BF16),
                   jax.ShapeDtypeStruct((n_mean, N_HEADS, HEAD_DIM), F32)],
        grid_spec=pltpu.PrefetchScalarGridSpec(
            num_scalar_prefetch=1,
            grid=(b, N_HEADS // hp, nb),
            in_specs=[qspec, kvspec, kvspec, tile, tile,
                      pl.BlockSpec((hp, 1, LANES), lambda bi, h, qi, pt: (h, 0, 0)), anyspec],
            out_specs=[qspec, anyspec],
            scratch_shapes=[pltpu.VMEM((hp, (nb + PAD_BLOCKS) * blk, AUG), BF16),
                            pltpu.VMEM((hp, nb + PAD_BLOCKS, HEAD_DIM, blk), BF16),
                            pltpu.VMEM((hp, nb, HEAD_DIM), F32),
                            pltpu.VMEM((hp, AUG, blk), BF16),
                            pltpu.VMEM((hp, blk, blk), F32),
                            pltpu.VMEM((hp, blk, blk), BF16),
                            pltpu.VMEM((hp, HEAD_DIM, blk), F32),
                            pltpu.VMEM((MEAN_DEPTH, ppb, PAGE_SIZE, N_HEADS, HEAD_DIM), F32),
                            pltpu.VMEM((3, MEAN_GROUP, N_HEADS, HEAD_DIM), F32),
                            pltpu.SMEM((1,), I32),
                            pltpu.SemaphoreType.DMA((MEAN_DEPTH,)),
                            pltpu.SemaphoreType.DMA((2,))]),
        compiler_params=_params(("arbitrary", "arbitrary", "arbitrary")),
        name="moba_prompt",
    )(page_table, q, k, v, own, adj, far, cache_k)
    return out, means.reshape(db, n_pages // ppb, N_HEADS, HEAD_DIM)


def _sample_select_body(q_ref, bm_ref, sel_ref):
    nblk = bm_ref.shape[1] // N_HEADS
    q = q_ref[0]
    for h in range(N_HEADS):
        qh = q[:, h * HEAD_DIM:(h + 1) * HEAD_DIM]
        bmh = bm_ref[0, pl.ds(h, nblk, stride=N_HEADS), :]
        g = lax.dot_general(qh, bmh, NT_DIMS, precision=lax.Precision.HIGHEST,
                            preferred_element_type=F32)
        n_iota = lax.broadcasted_iota(I32, g.shape, 1).astype(F32)
        lane = lax.broadcasted_iota(I32, (g.shape[0], LANES), 1)
        out = jnp.zeros((g.shape[0], LANES), I32)
        for r in range(MOBA_TOPK):
            mx = jnp.max(g, axis=1, keepdims=True)
            idx = jnp.min(jnp.where(g == mx, n_iota, float(nblk)), axis=1, keepdims=True)
            out = jnp.where(lane == r, idx.astype(I32), out)
            g = jnp.where(n_iota == idx, -jnp.inf, g)
        sel_ref[0, h] = out


def _sample_select(q3, blk_mean):
    db, t, d = q3.shape
    nblk = blk_mean.shape[1]
    bm2 = blk_mean.reshape(db, nblk * N_HEADS, HEAD_DIM)
    return pl.pallas_call(
        _sample_select_body,
        out_shape=jax.ShapeDtypeStruct((db, N_HEADS, t, LANES), I32),
        grid=(db,),
        in_specs=[pl.BlockSpec((1, t, d), lambda b: (b, 0, 0)),
                  pl.BlockSpec((1, nblk * N_HEADS, HEAD_DIM), lambda b: (b, 0, 0))],
        out_specs=pl.BlockSpec((1, N_HEADS, t, LANES), lambda b: (b, 0, 0, 0)),
        compiler_params=_params(("arbitrary",)),
        name="sample_select",
    )(q3, bm2)


GATHER_SLOTS = 4


def _sample_attn_body(layer, past_len, t_len, pt_ref, sel_ref, q_ref, kn_ref, vn_ref, far_ref,
                      last_ref, rb_ref, ck_hbm, cv_hbm, o_ref, kbuf, vbuf, sem):
    bi = pl.program_id(0)
    h = pl.program_id(1)
    nh = pl.num_programs(1)
    ppb = MOBA_BLOCK // PAGE_SIZE
    n_sel = t_len * MOBA_TOPK
    step = bi * nh + h
    n_steps = pl.num_programs(0) * nh
    slot = step % GATHER_SLOTS

    def sel_at(s_idx, j):
        return sel_ref[s_idx * n_sel + j]

    def copies(s_idx):
        b_idx = s_idx // nh
        h_idx = s_idx - b_idx * nh
        sl = s_idx % GATHER_SLOTS
        out = []
        for j in range(n_sel):
            blk_id = sel_at(s_idx, j)
            for pg in range(ppb):
                page = pt_ref[b_idx, blk_id * ppb + pg]
                dst = pl.ds(j * MOBA_BLOCK + pg * PAGE_SIZE, PAGE_SIZE)
                out.append(pltpu.make_async_copy(
                    ck_hbm.at[layer, page, :, h_idx, :], kbuf.at[sl, dst, :], sem.at[0, sl]))
                out.append(pltpu.make_async_copy(
                    cv_hbm.at[layer, page, :, h_idx, :], vbuf.at[sl, dst, :], sem.at[1, sl]))
        return out

    @pl.when(step == 0)
    def _():
        for ahead in range(GATHER_SLOTS - 1):
            for cp in copies(ahead):
                cp.start()

    @pl.when(step + GATHER_SLOTS - 1 < n_steps)
    def _():
        for cp in copies(step + GATHER_SLOTS - 1):
            cp.start()

    def rb_get(k):
        return rb_ref[h * N_BUCKETS + k] * LOG2E

    q = q_ref[0, 0:t_len, :] * (HEAD_DIM ** -0.5 * LOG2E)
    kn = kn_ref[0, 0:t_len, :]
    vn = vn_ref[0, 0:t_len, :]
    n_cols = n_sel * MOBA_BLOCK
    col = lax.broadcasted_iota(I32, (t_len, n_cols), 1)
    row = lax.broadcasted_iota(I32, (t_len, n_cols), 0)
    far = jnp.broadcast_to(jnp.concatenate([far_ref[0]] * (MOBA_BLOCK // LANES), axis=1),
                           (t_len, MOBA_BLOCK))
    last = last_ref[0, 0:t_len, :]
    last_block = past_len // MOBA_BLOCK - 1
    bias = jnp.concatenate(
        [jnp.where(sel_at(step, j) == last_block, last, far) for j in range(n_sel)], axis=1)
    per_q = MOBA_TOPK * MOBA_BLOCK
    mine = (col >= row * per_q) & (col < (row + 1) * per_q)
    bias = jnp.where(mine, bias, NEG)

    row1 = lax.broadcasted_iota(I32, (t_len, LANES), 0)
    s_own = []
    for j in range(t_len):
        d_o = row1 - j
        s_j = jnp.sum(q * kn[j:j + 1, :], axis=1, keepdims=True)
        s_own.append(jnp.where(d_o >= 0, s_j + _t5_bias(d_o, rb_get), NEG)[:, 0:1])

    for cp in copies(step):
        cp.wait()
    s = lax.dot_general(q.astype(BF16), kbuf[slot].astype(BF16), NT_DIMS,
                        preferred_element_type=F32) + bias
    m = jnp.max(s, axis=1, keepdims=True)
    for s_j in s_own:
        m = jnp.maximum(m, s_j)
    p = jnp.exp2(s - m)
    den = jnp.sum(p, axis=1, keepdims=True)
    acc = jnp.dot(p.astype(BF16), vbuf[slot].astype(BF16), preferred_element_type=F32)
    for j, s_j in enumerate(s_own):
        p_j = jnp.exp2(s_j - m)
        den = den + p_j
        acc = acc + p_j * vn[j:j + 1, :]
    o_ref[0] = jnp.zeros(o_ref.shape[1:], F32)
    o_ref[0, 0:t_len, :] = acc / den


def _sample_attn(q3, k3, v3, far, last, cache_k, cache_v, page_table, sel_flat, rb_flat, layer, t):
    db, rows, d = q3.shape
    past_len = page_table.shape[1] * PAGE_SIZE
    assert past_len % MOBA_BLOCK == 0 and t <= SUBLANES and db * N_HEADS >= GATHER_SLOTS
    n_cols = t * MOBA_TOPK * MOBA_BLOCK
    spec = pl.BlockSpec((1, rows, HEAD_DIM), lambda b, h, pt, sl: (b, 0, h))
    farspec = pl.BlockSpec((1, 1, LANES), lambda b, h, pt, sl: (h, 0, 0))
    lastspec = pl.BlockSpec((1, SUBLANES, MOBA_BLOCK), lambda b, h, pt, sl: (h, 0, 0))
    anyspec = pl.BlockSpec(memory_space=pl.ANY)
    return pl.pallas_call(
        functools.partial(_sample_attn_body, layer, past_len, t),
        out_shape=jax.ShapeDtypeStruct((db, rows, d), F32),
        grid_spec=pltpu.PrefetchScalarGridSpec(
            num_scalar_prefetch=2,
            grid=(db, N_HEADS),
            in_specs=[spec, spec, spec, farspec, lastspec,
                      pl.BlockSpec(memory_space=pltpu.SMEM), anyspec, anyspec],
            out_specs=spec,
            scratch_shapes=[pltpu.VMEM((GATHER_SLOTS, n_cols, HEAD_DIM), F32),
                            pltpu.VMEM((GATHER_SLOTS, n_cols, HEAD_DIM), F32),
                            pltpu.SemaphoreType.DMA((2, GATHER_SLOTS))]),
        compiler_params=_params(("arbitrary", "arbitrary")),
        name="sample_attn",
    )(page_table, sel_flat, q3, k3, v3, far, last, rb_flat, cache_k, cache_v)


SUB = 8


def _gla_body(q_ref, k_ref, v_ref, lf_ref, gs_ref, gn_ref, s0_ref, o_ref, sfin_ref,
              st_scr, b_scr):
    ci = pl.program_id(1)
    chunk = q_ref.shape[0]
    nsub = chunk // SUB

    @pl.when(ci == 0)
    def _():
        for h in range(HG_HEADS):
            st_scr[h] = s0_ref[0, h].T

    r_i = lax.broadcasted_iota(I32, (chunk, chunk), 0)
    c_i = lax.broadcasted_iota(I32, (chunk, chunk), 1)
    tri = jnp.where(r_i >= c_i, 1.0, 0.0).astype(F32)
    b_all = jnp.dot(tri, lf_ref[...], precision=lax.Precision.HIGHEST,
                    preferred_element_type=F32) * LOG2E
    b_scr[...] = b_all
    neg_inf = -jnp.inf
    t_sub = lax.broadcasted_iota(I32, (SUB, HG_DIM), 0)
    causal = [t_sub >= s for s in range(SUB)]

    def zeros(n):
        return [jnp.zeros((n, HG_DIM), F32)] if n else []

    def cols(h):
        return slice(h * HG_DIM, (h + 1) * HG_DIM)


    o_state, att = [], []
    for h in range(HG_HEADS):
        sl = cols(h)
        b, q, k = b_all[:, sl], q_ref[:, sl], k_ref[:, sl]
        o_state.append(lax.dot_general((q * jnp.exp2(b)).astype(BF16),
                                       st_scr[h].astype(BF16), NT_DIMS,
                                       preferred_element_type=F32))
        if nsub > 1:
            qs, ks = [], []
            for j in range(nsub - 1):
                lo, hi = SUB * j, SUB * (j + 1)
                e_j = b[hi - 1:hi, :]
                qs.append(jnp.concatenate(
                    zeros(hi) + [q[hi:] * jnp.exp2(b[hi:] - e_j)], axis=0))
                ks.append(jnp.concatenate(
                    zeros(lo) + [k[lo:hi] * jnp.exp2(e_j - b[lo:hi])] + zeros(chunk - hi), axis=0))
            att.append(lax.dot_general(jnp.concatenate(qs, axis=1).astype(BF16),
                                       jnp.concatenate(ks, axis=1).astype(BF16), NT_DIMS,
                                       preferred_element_type=F32))

    for h in range(HG_HEADS):
        sl = cols(h)
        b = b_all[:, sl]
        b_last = b[chunk - 1:chunk, :]
        ke = k_ref[:, sl] * jnp.exp2(b_last - b)
        st_scr[h] = st_scr[h] * jnp.exp2(b_last) + lax.dot_general(
            v_ref[:, sl].astype(BF16), ke.astype(BF16), TN_DIMS, preferred_element_type=F32)

    o_diag = []
    for h in range(HG_HEADS):
        sl = cols(h)
        diag = []
        for i in range(nsub):
            r0 = SUB * i
            b_i, q_i = b_all[r0:r0 + SUB, sl], q_ref[r0:r0 + SUB, sl]
            o_i = jnp.zeros((SUB, HG_DIM), F32)
            for s in range(SUB):
                r = r0 + s
                dec = jnp.exp2(jnp.where(causal[s], b_i - b_scr[r:r + 1, sl], neg_inf))
                a = jnp.sum(dec * q_i * k_ref[r:r + 1, sl], axis=1, keepdims=True)
                o_i = o_i + a * v_ref[r:r + 1, sl]
            diag.append(o_i)
        o_diag.append(jnp.concatenate(diag, axis=0) if nsub > 1 else diag[0])

    for h in range(HG_HEADS):
        sl = cols(h)
        o = o_state[h] + o_diag[h]
        if nsub > 1:
            o = o + jnp.dot(att[h].astype(BF16), v_ref[:, sl].astype(BF16),
                            preferred_element_type=F32)
        o = o * lax.rsqrt(jnp.mean(o * o, axis=-1, keepdims=True) + EPS)
        o_ref[:, sl] = (o * gn_ref[:, sl] * gs_ref[:, sl]).astype(o_ref.dtype)

    @pl.when(ci == pl.num_programs(1) - 1)
    def _():
        for h in range(HG_HEADS):
            sfin_ref[0, h] = st_scr[h].T


def _gla(q, k, v, lf, gs, g_norm, s0, chunk):
    m, d = q.shape
    nb = s0.shape[0]
    nc = m // nb // chunk
    row = pl.BlockSpec((chunk, d), lambda b, c: (b * nc + c, 0))
    state = pl.BlockSpec((1, HG_HEADS, HG_DIM, HG_DIM), lambda b, c: (b, 0, 0, 0))
    return pl.pallas_call(
        _gla_body,
        out_shape=[jax.ShapeDtypeStruct((m, d), BF16),
                   jax.ShapeDtypeStruct(s0.shape, F32)],
        grid=(nb, nc),
        in_specs=[row, row, row, row, row, pl.BlockSpec((1, d), lambda b, c: (0, 0)), state],
        out_specs=[row, state],
        scratch_shapes=[pltpu.VMEM((HG_HEADS, HG_DIM, HG_DIM), F32),
                        pltpu.VMEM((chunk, d), F32)],
        compiler_params=_params(("arbitrary", "arbitrary")),
        name="gla",
    )(q, k, v, lf, gs, g_norm.reshape(1, d), s0)


PROMPT_TM = 512
MLP_TM = 1024
MLP_TF = 1024
GLA_CHUNK = 64
GLA_SAMPLE_CHUNK = 16


def kernel(x_prompt, x_sample, cache_k, cache_v, state_hgrn, page_table, norm_mix, norm_ffn,
           norm_final, attn_w_qkv, attn_w_o, rel_bias, hg_w_in, hg_lb_param, hg_norm, hg_w_o,
           ffn_w1, ffn_w2):
    b, s, d = x_prompt.shape
    db, t, _ = x_sample.shape
    tp = GLA_SAMPLE_CHUNK
    mp, ms = b * s, db * tp
    xp = x_prompt.reshape(mp, d)
    xs = jnp.pad(x_sample, ((0, 0), (0, tp - t), (0, 0))).reshape(ms, d)
    rb_flat = rel_bias.T.reshape(-1)
    w_qkv = _to_bf16(attn_w_qkv)
    w_ao = _to_bf16(attn_w_o)
    w_in = _to_bf16(hg_w_in)
    w_ho = _to_bf16(hg_w_o)
    w1 = _to_bf16(ffn_w1)
    w2 = _to_bf16(ffn_w2)

    own, adj, far, last = _bias_tables(rb_flat)
    qp, kp, vp = _qkv_proj(xp, norm_mix[0], w_qkv[0], PROMPT_TM)
    op, blk_mean = _moba_prompt(qp.reshape(b, s, d), kp.reshape(b, s, d), vp.reshape(b, s, d),
                                own, adj, far, cache_k, page_table, 0)
    xp = _proj_mlp(xp, op.reshape(mp, d), w_ao[0], norm_ffn[0], w1[0], w2[0], norm_final,
                   False, MLP_TM, MLP_TF)

    qs, ks, vs = _qkv_proj(xs, norm_mix[0], w_qkv[0], ms)
    q3 = qs.reshape(db, tp, d)
    sel = _sample_select(q3, blk_mean)
    sel_flat = sel[:, :, :t, :MOBA_TOPK].reshape(-1)
    os_ = _sample_attn(q3, ks.reshape(db, tp, d), vs.reshape(db, tp, d), far, last,
                       cache_k, cache_v, page_table, sel_flat, rb_flat, 0, t)
    xs = _proj_mlp(xs, os_.reshape(ms, d).astype(BF16), w_ao[0], norm_ffn[0], w1[0], w2[0],
                   norm_final, False, ms, MLP_TF)

    hq, hk, hv, hlf, hgs = _hgrn_proj(xp, norm_mix[1], w_in[0], hg_lb_param, 1, PROMPT_TM)
    s0p = jnp.zeros((b, HG_HEADS, HG_DIM, HG_DIM), F32)
    hop, s_prompt = _gla(hq, hk, hv, hlf, hgs, hg_norm[0], s0p, GLA_CHUNK)
    y_prompt = _proj_mlp(xp, hop, w_ho[0], norm_ffn[1], w1[1], w2[1], norm_final,
                         True, MLP_TM, MLP_TF)

    outs = _hgrn_proj(xs, norm_mix[1], w_in[0], hg_lb_param, 1, ms, (tp, t))
    hos, s_sample = _gla(*outs, hg_norm[0], state_hgrn[0], tp)
    y_sample = _proj_mlp(xs, hos, w_ho[0], norm_ffn[1], w1[1], w2[1], norm_final,
                         True, ms, MLP_TF)

    hd = (N_HEADS, HEAD_DIM)
    return (y_prompt.reshape(b, s, d), y_sample.reshape(db, tp, d)[:, :t],
            kp.reshape(1, b, s, *hd), vp.reshape(1, b, s, *hd),
            ks.reshape(1, db, tp, *hd)[:, :, :t], vs.reshape(1, db, tp, *hd)[:, :, :t],
            s_prompt[None], s_sample[None])
```

```python
import functools
import math

import jax
import jax.numpy as jnp
from jax import lax
from jax.experimental import pallas as pl
from jax.experimental.pallas import tpu as pltpu

F32 = jnp.float32
BF16 = jnp.bfloat16
I32 = jnp.int32

N_HEADS = 8
HEAD_DIM = 128
MOBA_BLOCK = 256
MOBA_TOPK = 3
PAGE_SIZE = 128
N_BUCKETS = 32
MAX_DISTANCE = 128
MAX_EXACT = N_BUCKETS // 2
HG_HEADS = 8
HG_DIM = 128
EPS = 1e-6

LANES = 128
SUBLANES = 8
VMEM_LIMIT = 56 * 1024 * 1024

NEG = -1e30
LOG2E = math.log2(math.e)

NT_DIMS = (((1,), (1,)), ((), ()))
TN_DIMS = (((0,), (0,)), ((), ()))


def _params(sem):
    return pltpu.CompilerParams(dimension_semantics=sem, vmem_limit_bytes=VMEM_LIMIT)


def _rms(x, g):
    ms = jnp.mean(x * x, axis=-1, keepdims=True)
    return x * lax.rsqrt(ms + EPS) * g


def _t5_bias(dist, rb_get):
    n = jnp.maximum(dist, 0)
    nf = jnp.maximum(n, MAX_EXACT).astype(F32)
    large = MAX_EXACT + jnp.floor(jnp.log(nf / MAX_EXACT) / math.log(MAX_DISTANCE / MAX_EXACT)
                                  * (N_BUCKETS - MAX_EXACT)).astype(I32)
    large = jnp.minimum(large, N_BUCKETS - 1)
    bucket = jnp.where(n < MAX_EXACT, n, large)
    out = jnp.zeros(dist.shape, F32)
    for k in range(N_BUCKETS):
        out = jnp.where(bucket == k, rb_get(k), out)
    return out


CAST_BLOCK_BYTES = 4 * 1024 * 1024
CAST_BLOCK_COLS = 1024


def _cast_body(x_ref, o_ref):
    o_ref[...] = x_ref[...].astype(o_ref.dtype)


def _to_bf16(w):
    cols = w.shape[-1]
    rows = math.prod(w.shape[:-1])
    bc = min(cols, CAST_BLOCK_COLS)
    budget_rows = max(2 * SUBLANES, CAST_BLOCK_BYTES // (bc * 4))
    br = min(rows, 1 << (budget_rows.bit_length() - 1))
    assert rows % br == 0 and cols % bc == 0
    spec = pl.BlockSpec((br, bc), lambda i, j: (i, j))
    out = pl.pallas_call(
        _cast_body,
        out_shape=jax.ShapeDtypeStruct((rows, cols), BF16),
        grid=(rows // br, cols // bc),
        in_specs=[spec],
        out_specs=spec,
        compiler_params=_params(("arbitrary", "arbitrary")),
        name="to_bf16",
    )(w.reshape(rows, cols))
    return out.reshape(w.shape)


def _qkv_body(x_ref, g_ref, w_ref, q_ref, k_ref, v_ref):
    d = x_ref.shape[1]
    xn = _rms(x_ref[...], g_ref[...]).astype(BF16)
    for c, o_ref in enumerate((q_ref, k_ref, v_ref)):
        o_ref[...] = jnp.dot(xn, w_ref[:, c * d:(c + 1) * d], preferred_element_type=F32)


def _qkv_proj(x, g, w_bf16, tm):
    m, d = x.shape
    row = pl.BlockSpec((tm, d), lambda i: (i, 0))
    return pl.pallas_call(
        _qkv_body,
        out_shape=[jax.ShapeDtypeStruct((m, d), F32)] * 3,
        grid=(m // tm,),
        in_specs=[row, pl.BlockSpec((1, d), lambda i: (0, 0)),
                  pl.BlockSpec((d, 3 * d), lambda i: (0, 0))],
        out_specs=[row] * 3,
        compiler_params=_params(("arbitrary",)),
        name="qkv_proj",
    )(x, g.reshape(1, d), w_bf16)


def _hgrn_proj_body(layer, seq_rows, x_ref, g_ref, w_ref, lbp_ref,
                    q_ref, k_ref, v_ref, lf_ref, gs_ref):
    d = x_ref.shape[1]
    xn = _rms(x_ref[...], g_ref[...]).astype(BF16)

    def proj(c):
        return jnp.dot(xn, w_ref[:, c * d:(c + 1) * d], preferred_element_type=F32)

    lbp = lbp_ref[...]
    e = jnp.exp(lbp - jnp.max(lbp, axis=0, keepdims=True))
    sm = e / jnp.sum(e, axis=0, keepdims=True)
    cum = sm[0:1]
    for r in range(1, layer + 1):
        cum = cum + sm[r:r + 1]
    lb = cum - sm[0:1]

    q_pre = proj(0)
    q_ref[...] = q_pre * jax.nn.sigmoid(q_pre) * (HG_DIM ** -0.5)
    fgate = lb + (1.0 - lb) * jax.nn.sigmoid(proj(1))
    if seq_rows is not None:
        row = lax.broadcasted_iota(I32, fgate.shape, 0)
        fgate = jnp.where((row & (seq_rows[0] - 1)) < seq_rows[1], fgate, 1.0)
    lf_ref[...] = jnp.log(fgate)
    k_ref[...] = 1.0 - fgate
    v_ref[...] = proj(2)
    g_pre = proj(3)
    gs_ref[...] = g_pre * jax.nn.sigmoid(g_pre)


def _hgrn_proj(x, g, w_bf16, lb_param, layer, tm, seq_rows=None):
    m, d = x.shape
    depth = lb_param.shape[0]
    assert seq_rows is None or (tm % seq_rows[0] == 0 and seq_rows[0] & (seq_rows[0] - 1) == 0)
    row = pl.BlockSpec((tm, d), lambda i: (i, 0))
    return pl.pallas_call(
        functools.partial(_hgrn_proj_body, layer, seq_rows),
        out_shape=[jax.ShapeDtypeStruct((m, d), F32)] * 5,
        grid=(m // tm,),
        in_specs=[row, pl.BlockSpec((1, d), lambda i: (0, 0)),
                  pl.BlockSpec((d, 4 * d), lambda i: (0, 0)),
                  pl.BlockSpec((depth, d), lambda i: (0, 0))],
        out_specs=[row] * 5,
        compiler_params=_params(("arbitrary",)),
        name="hgrn_proj",
    )(x, g.reshape(1, d), w_bf16, lb_param)


def _proj_mlp_body(final_norm, x_ref, o_ref, wo_ref, g_ref, w1_ref, w2_ref, gf_ref,
                   out_ref, x1_scr, xn_scr):
    f = pl.program_id(1)

    @pl.when(f == 0)
    def _():
        x1 = x_ref[...] + jnp.dot(o_ref[...], wo_ref[...], preferred_element_type=F32)
        x1_scr[...] = x1
        xn_scr[...] = _rms(x1, g_ref[...]).astype(BF16)
        out_ref[...] = jnp.zeros_like(out_ref)

    h = jnp.maximum(jnp.dot(xn_scr[...], w1_ref[...], preferred_element_type=F32), 0.0)
    out_ref[...] += jnp.dot((h * h).astype(BF16), w2_ref[...], preferred_element_type=F32)

    @pl.when(f == pl.num_programs(1) - 1)
    def _():
        y = x1_scr[...] + out_ref[...]
        if final_norm:
            y = _rms(y, gf_ref[...])
        out_ref[...] = y


def _proj_mlp(x, o_bf16, wo, g_ffn, w1_all, w2_all, layer, g_final, final_norm, tm, tf):
    m, d = x.shape
    dff = w1_all.shape[2]
    row = pl.BlockSpec((tm, d), lambda i, f: (i, 0))
    vec = pl.BlockSpec((1, d), lambda i, f: (0, 0))
    return pl.pallas_call(
        functools.partial(_proj_mlp_body, final_norm),
        out_shape=jax.ShapeDtypeStruct((m, d), F32),
        grid=(m // tm, dff // tf),
        in_specs=[row, row, pl.BlockSpec((d, d), lambda i, f: (0, 0)), vec,
                  pl.BlockSpec((None, d, tf), lambda i, f: (layer, 0, f)),
                  pl.BlockSpec((None, tf, d), lambda i, f: (layer, f, 0)), vec],
        out_specs=row,
        scratch_shapes=[pltpu.VMEM((tm, d), F32), pltpu.VMEM((tm, d), BF16)],
        compiler_params=_params(("arbitrary", "arbitrary")),
        name="proj_mlp",
    )(x, o_bf16, wo, g_ffn.reshape(1, d), w1_all, w2_all, g_final.reshape(1, d))


def _bias_tables_body(rb_ref, own_ref, adj_ref, far_ref, last_ref):
    h = pl.program_id(0)
    shape = (MOBA_BLOCK, MOBA_BLOCK)
    d_own = lax.broadcasted_iota(I32, shape, 1) - lax.broadcasted_iota(I32, shape, 0)

    def rb_get(k):
        return rb_ref[h * N_BUCKETS + k] * LOG2E

    own_ref[0] = jnp.where(d_own >= 0, _t5_bias(d_own, rb_get), NEG)
    adj_ref[0] = _t5_bias(d_own + MOBA_BLOCK, rb_get)
    d_far = jnp.full((1, LANES), MOBA_BLOCK + 1, I32)
    far_ref[0] = _t5_bias(d_far, rb_get)
    lshape = (SUBLANES, MOBA_BLOCK)
    d_last = (MOBA_BLOCK + lax.broadcasted_iota(I32, lshape, 0)
              - lax.broadcasted_iota(I32, lshape, 1))
    last_ref[0] = _t5_bias(d_last, rb_get)


def _bias_tables(rb_flat):
    tile = pl.BlockSpec((1, MOBA_BLOCK, MOBA_BLOCK), lambda h: (h, 0, 0))
    return pl.pallas_call(
        _bias_tables_body,
        out_shape=[jax.ShapeDtypeStruct((N_HEADS, MOBA_BLOCK, MOBA_BLOCK), F32),
                   jax.ShapeDtypeStruct((N_HEADS, MOBA_BLOCK, MOBA_BLOCK), F32),
                   jax.ShapeDtypeStruct((N_HEADS, 1, LANES), F32),
                   jax.ShapeDtypeStruct((N_HEADS, SUBLANES, MOBA_BLOCK), F32)],
        grid=(N_HEADS,),
        in_specs=[pl.BlockSpec(memory_space=pltpu.SMEM)],
        out_specs=[tile, tile, pl.BlockSpec((1, 1, LANES), lambda h: (h, 0, 0)),
                   pl.BlockSpec((1, SUBLANES, MOBA_BLOCK), lambda h: (h, 0, 0))],
        compiler_params=_params(("arbitrary",)),
        name="bias_tables",
    )(rb_flat)


STAGE_BLOCKS = 2
PAD_BLOCKS = 2 * STAGE_BLOCKS - 1
NB_MAX = 32
AUG = 2 * HEAD_DIM
FLAG_COL = 2 * NB_MAX
HEADS_PER_STEP = 4


MEAN_DEPTH = 8
MEAN_GROUP = 16
MEAN_PER_STEP = 2
STAGE_SLOTS = 3


def _moba_prompt_body(layer, pt_ref, q_ref, k_ref, v_ref, own_ref, adj_ref, far_ref, ck_hbm,
                      o_ref, bm_hbm, kaug, vt, kmean, qaug, s_scr, p_scr, acc_s,
                      ring, stage, count, sem_in, sem_out):
    qi = pl.program_id(2)
    nb = kmean.shape[1]
    blk = MOBA_BLOCK
    heads = range(HEADS_PER_STEP)
    ppb = MOBA_BLOCK // PAGE_SIZE
    blocks_per_seq = pt_ref.shape[1] // ppb
    n_mean = pt_ref.shape[0] * blocks_per_seq
    grid_pos = (pl.program_id(0) * pl.num_programs(1) + pl.program_id(1)) * pl.num_programs(2) + qi
    n_steps = pl.num_programs(0) * pl.num_programs(1) * pl.num_programs(2)

    def hcols(i):
        return slice(i * HEAD_DIM, (i + 1) * HEAD_DIM)

    def page_copies(n):
        seq = n // blocks_per_seq
        first_page = (n - seq * blocks_per_seq) * ppb
        slot = n % MEAN_DEPTH
        return [pltpu.make_async_copy(ck_hbm.at[layer, pt_ref[seq, first_page + pg]],
                                      ring.at[slot, pg], sem_in.at[slot]) for pg in range(ppb)]

    def group_copy(g):
        slot = g % STAGE_SLOTS
        return pltpu.make_async_copy(
            stage.at[slot], bm_hbm.at[pl.ds(pl.multiple_of(g * MEAN_GROUP, MEAN_GROUP), MEAN_GROUP)],
            sem_out.at[slot])

    def reduce_block(n):
        slot = n % MEAN_DEPTH
        tot = jnp.sum(ring[slot, 0], axis=0)
        for pg in range(1, ppb):
            tot = tot + jnp.sum(ring[slot, pg], axis=0)
        g = n // MEAN_GROUP
        stage[jnp.where(n < n_mean, g % STAGE_SLOTS, STAGE_SLOTS), n - g * MEAN_GROUP] = (
            tot * (1.0 / MOBA_BLOCK))

    def advance_stream(n):
        @pl.when(n + MEAN_DEPTH < n_mean)
        def _():
            for cp in page_copies(n + MEAN_DEPTH):
                cp.start()

        @pl.when(n + MEAN_PER_STEP < n_mean)
        def _():
            for cp in page_copies(n + MEAN_PER_STEP):
                cp.wait()

        g = n // MEAN_GROUP

        @pl.when((n - g * MEAN_GROUP == MEAN_GROUP - 1) & (n < n_mean))
        def _():
            @pl.when(g >= 1)
            def _():
                group_copy(g - 1).wait()
            group_copy(g).start()

    @pl.when(grid_pos == 0)
    def _():
        count[0] = 0
        for n in range(MEAN_DEPTH):
            for cp in page_copies(n):
                cp.start()
        for n in range(MEAN_PER_STEP):
            for cp in page_copies(n):
                cp.wait()

    @pl.when(qi == 0)
    def _():
        lane_p = lax.broadcasted_iota(I32, (PAD_BLOCKS * blk, AUG), 1)
        pad_keys = jnp.where(lane_p == HEAD_DIM + FLAG_COL, 1.0, 0.0).astype(BF16)
        row_q = lax.broadcasted_iota(I32, (AUG - HEAD_DIM, blk), 0)
        for i in heads:
            kaug[i, 0:PAD_BLOCKS * blk, :] = pad_keys
            for n in range(PAD_BLOCKS):
                vt[i, n] = jnp.zeros((HEAD_DIM, blk), BF16)
            kmean[i] = jnp.zeros((nb, HEAD_DIM), F32)
            qaug[i, HEAD_DIM:AUG, :] = jnp.where(row_q == FLAG_COL, NEG, 0.0).astype(BF16)

    lane = lax.broadcasted_iota(I32, (blk, LANES), 1)
    hot = jnp.where(lane == qi, 1.0, jnp.where(lane == NB_MAX + qi, 1.0, 0.0)).astype(BF16)
    row_n = lax.broadcasted_iota(I32, (nb, HEAD_DIM), 0)
    own_rows = pl.ds(pl.multiple_of((qi + PAD_BLOCKS) * blk, blk), blk)
    for i in heads:
        kblk = k_ref[0, :, hcols(i)]
        kaug[i, own_rows, 0:HEAD_DIM] = kblk.astype(BF16)
        kaug[i, own_rows, HEAD_DIM:AUG] = hot
        vt[i, qi + PAD_BLOCKS] = v_ref[0, :, hcols(i)].T.astype(BF16)
        kmean[i] = jnp.where(row_n == qi, jnp.mean(kblk, axis=0, keepdims=True), kmean[i])

    qf = qi.astype(F32)

    gate = [lax.dot_general(kmean[i], q_ref[0, :, hcols(i)], NT_DIMS,
                            precision=lax.Precision.HIGHEST, preferred_element_type=F32)
            for i in heads]
    for i in heads:
        qaug[i, 0:HEAD_DIM, :] = (
            q_ref[0, :, hcols(i)] * (HEAD_DIM ** -0.5 * LOG2E)).T.astype(BF16)
    n_io = lax.broadcasted_iota(I32, gate[0].shape, 0).astype(F32)
    g = [jnp.where(n_io < qf, gate[i], -jnp.inf) for i in heads]
    sel = [jnp.zeros(gate[0].shape, F32) for i in heads]
    for _ in range(MOBA_TOPK):
        mx = [jnp.max(g[i], axis=0, keepdims=True) for i in heads]
        idx = [jnp.min(jnp.where(g[i] == mx[i], n_io, float(nb)), axis=0, keepdims=True)
               for i in heads]
        hit = [n_io == idx[i] for i in heads]
        sel = [jnp.where(hit[i], 1.0, sel[i]) for i in heads]
        g = [jnp.where(hit[i], -jnp.inf, g[i]) for i in heads]
    for i in heads:
        far = jnp.concatenate([far_ref[i]] * (blk // LANES), axis=1)
        picked = jnp.where(sel[i] > 0.0, jnp.where(n_io < qf - 1.0, far, 0.0), NEG)
        bias = jnp.where(n_io < qf, picked, jnp.where(n_io == qf, 0.0, NEG))
        bias_hi = bias.astype(BF16)
        qaug[i, HEAD_DIM:HEAD_DIM + nb, :] = bias_hi
        qaug[i, HEAD_DIM + NB_MAX:HEAD_DIM + NB_MAX + nb, :] = (
            bias - bias_hi.astype(F32)).astype(BF16)

    def stage_first(j):
        return qi - (STAGE_BLOCKS - 1) - STAGE_BLOCKS * j + PAD_BLOCKS

    def scores(i, first):
        r0 = pl.multiple_of(first * blk, blk)
        return jnp.dot(kaug[i, pl.ds(r0, STAGE_BLOCKS * blk), :], qaug[i],
                       preferred_element_type=F32)

    def values(i, first, p):
        out = jnp.dot(vt[i, first], p[0:blk], preferred_element_type=F32)
        for c in range(1, STAGE_BLOCKS):
            out = out + jnp.dot(vt[i, first + c], p[c * blk:(c + 1) * blk],
                                preferred_element_type=F32)
        return out

    for i in heads:
        s = scores(i, stage_first(0))
        s_scr[i] = jnp.concatenate(
            [s[0:blk] + adj_ref[i], s[blk:STAGE_BLOCKS * blk] + own_ref[i]], axis=0)
        acc_s[i] = jnp.zeros((HEAD_DIM, blk), F32)

    n0 = count[0]
    reduce_block(n0)
    carry0 = []
    for i in heads:
        s_next = scores(i, stage_first(1))
        s_cur = s_scr[i]
        m0 = jnp.max(s_cur, axis=0, keepdims=True)
        p = jnp.exp2(s_cur - m0)
        p_scr[i] = p.astype(BF16)
        s_scr[i] = s_next
        carry0 += [m0, jnp.sum(p, axis=0, keepdims=True), jnp.ones_like(m0)]
    advance_stream(n0)

    def step(t, carry):
        n = carry[-1]
        for c in range(MEAN_PER_STEP):
            reduce_block(n + c)
        out = []
        for i in heads:
            m_prev, l_prev, alpha_prev = carry[3 * i:3 * i + 3]
            s_next = scores(i, stage_first(t + 1))
            pv = values(i, stage_first(t - 1), p_scr[i])
            acc_s[i] = alpha_prev * acc_s[i] + pv
            s_cur = s_scr[i]
            m_new = jnp.maximum(m_prev, jnp.max(s_cur, axis=0, keepdims=True))
            alpha = jnp.exp2(m_prev - m_new)
            p = jnp.exp2(s_cur - m_new)
            p_scr[i] = p.astype(BF16)
            s_scr[i] = s_next
            out += [m_new, alpha * l_prev + jnp.sum(p, axis=0, keepdims=True), alpha]
        for c in range(MEAN_PER_STEP):
            advance_stream(n + c)
        return tuple(out) + (n + MEAN_PER_STEP,)

    n_stages = qi // STAGE_BLOCKS + 1
    fin = lax.fori_loop(1, n_stages, step, tuple(carry0) + (n0 + 1,))
    for i in heads:
        pv = values(i, stage_first(n_stages - 1), p_scr[i])
        acc = fin[3 * i + 2] * acc_s[i] + pv
        o_ref[0, :, hcols(i)] = (acc / fin[3 * i + 1]).T.astype(o_ref.dtype)
    count[0] = fin[-1]

    @pl.when(grid_pos == n_steps - 1)
    def _():
        def drain(n, carry):
            reduce_block(n)
            advance_stream(n)
            return carry

        lax.fori_loop(fin[-1], n_mean, drain, 0)
        group_copy(n_mean // MEAN_GROUP - 1).wait()


def _moba_prompt(q, k, v, own, adj, far, cache_k, page_table, layer):
    b, s, d = q.shape
    nb = s // MOBA_BLOCK
    assert nb <= NB_MAX and FLAG_COL < AUG - HEAD_DIM and N_HEADS % HEADS_PER_STEP == 0
    blk = MOBA_BLOCK
    hp = HEADS_PER_STEP
    ppb = MOBA_BLOCK // PAGE_SIZE
    db, n_pages = page_table.shape
    n_mean = db * (n_pages // ppb)
    assert n_pages % ppb == 0 and n_mean % MEAN_GROUP == 0 and n_mean >= MEAN_DEPTH
    assert MEAN_DEPTH > MEAN_PER_STEP and STAGE_BLOCKS == 2
    qspec = pl.BlockSpec((1, blk, hp * HEAD_DIM), lambda bi, h, qi, pt: (bi, qi, h))
    kvspec = qspec
    tile = pl.BlockSpec((hp, blk, blk), lambda bi, h, qi, pt: (h, 0, 0))
    anyspec = pl.BlockSpec(memory_space=pl.ANY)
    out, means = pl.pallas_call(
        functools.partial(_moba_prompt_body, layer),
        out_shape=[jax.ShapeDtypeStruct((b, s, d), BF16),
                   jax.ShapeDtypeStruct((n_mean, N_HEADS, HEAD_DIM), F32)],
        grid_spec=pltpu.PrefetchScalarGridSpec(
            num_scalar_prefetch=1,
            grid=(b, N_HEADS // hp, nb),
            in_specs=[qspec, kvspec, kvspec, tile, tile,
                      pl.BlockSpec((hp, 1, LANES), lambda bi, h, qi, pt: (h, 0, 0)), anyspec],
            out_specs=[qspec, anyspec],
            scratch_shapes=[pltpu.VMEM((hp, (nb + PAD_BLOCKS) * blk, AUG), BF16),
                            pltpu.VMEM((hp, nb + PAD_BLOCKS, HEAD_DIM, blk), BF16),
                            pltpu.VMEM((hp, nb, HEAD_DIM), F32),
                            pltpu.VMEM((hp, AUG, blk), BF16),
                            pltpu.VMEM((hp, STAGE_BLOCKS * blk, blk), F32),
                            pltpu.VMEM((hp, STAGE_BLOCKS * blk, blk), BF16),
                            pltpu.VMEM((hp, HEAD_DIM, blk), F32),
                            pltpu.VMEM((MEAN_DEPTH, ppb, PAGE_SIZE, N_HEADS, HEAD_DIM), F32),
                            pltpu.VMEM((STAGE_SLOTS + 1, MEAN_GROUP, N_HEADS, HEAD_DIM), F32),
                            pltpu.SMEM((1,), I32),
                            pltpu.SemaphoreType.DMA((MEAN_DEPTH,)),
                            pltpu.SemaphoreType.DMA((STAGE_SLOTS,))]),
        compiler_params=_params(("arbitrary", "arbitrary", "arbitrary")),
        name="moba_prompt",
    )(page_table, q, k, v, own, adj, far, cache_k)
    return out, means.reshape(db, n_pages // ppb, N_HEADS, HEAD_DIM)


def _sample_select_body(q_ref, bm_ref, sel_ref):
    nblk = bm_ref.shape[1] // N_HEADS
    q = q_ref[0]
    for h in range(N_HEADS):
        qh = q[:, h * HEAD_DIM:(h + 1) * HEAD_DIM]
        bmh = bm_ref[0, pl.ds(h, nblk, stride=N_HEADS), :]
        g = lax.dot_general(qh, bmh, NT_DIMS, precision=lax.Precision.HIGHEST,
                            preferred_element_type=F32)
        n_iota = lax.broadcasted_iota(I32, g.shape, 1).astype(F32)
        lane = lax.broadcasted_iota(I32, (g.shape[0], LANES), 1)
        out = jnp.zeros((g.shape[0], LANES), I32)
        for r in range(MOBA_TOPK):
            mx = jnp.max(g, axis=1, keepdims=True)
            idx = jnp.min(jnp.where(g == mx, n_iota, float(nblk)), axis=1, keepdims=True)
            out = jnp.where(lane == r, idx.astype(I32), out)
            g = jnp.where(n_iota == idx, -jnp.inf, g)
        sel_ref[0, h] = out


def _sample_select(q3, blk_mean):
    db, t, d = q3.shape
    nblk = blk_mean.shape[1]
    bm2 = blk_mean.reshape(db, nblk * N_HEADS, HEAD_DIM)
    return pl.pallas_call(
        _sample_select_body,
        out_shape=jax.ShapeDtypeStruct((db, N_HEADS, t, LANES), I32),
        grid=(db,),
        in_specs=[pl.BlockSpec((1, t, d), lambda b: (b, 0, 0)),
                  pl.BlockSpec((1, nblk * N_HEADS, HEAD_DIM), lambda b: (b, 0, 0))],
        out_specs=pl.BlockSpec((1, N_HEADS, t, LANES), lambda b: (b, 0, 0, 0)),
        compiler_params=_params(("arbitrary",)),
        name="sample_select",
    )(q3, bm2)


GATHER_SLOTS = 4


def _sample_attn_body(layer, past_len, t_len, pt_ref, sel_ref, q_ref, kn_ref, vn_ref, far_ref,
                      last_ref, rb_ref, ck_hbm, cv_hbm, o_ref, kbuf, vbuf, sem):
    bi = pl.program_id(0)
    h = pl.program_id(1)
    nh = pl.num_programs(1)
    ppb = MOBA_BLOCK // PAGE_SIZE
    n_sel = t_len * MOBA_TOPK
    step = bi * nh + h
    n_steps = pl.num_programs(0) * nh
    slot = step % GATHER_SLOTS

    def sel_at(s_idx, j):
        return sel_ref[s_idx * n_sel + j]

    def copies(s_idx):
        b_idx = s_idx // nh
        h_idx = s_idx - b_idx * nh
        sl = s_idx % GATHER_SLOTS
        out = []
        for j in range(n_sel):
            blk_id = sel_at(s_idx, j)
            for pg in range(ppb):
                page = pt_ref[b_idx, blk_id * ppb + pg]
                dst = pl.ds(j * MOBA_BLOCK + pg * PAGE_SIZE, PAGE_SIZE)
                out.append(pltpu.make_async_copy(
                    ck_hbm.at[layer, page, :, h_idx, :], kbuf.at[sl, dst, :], sem.at[0, sl]))
                out.append(pltpu.make_async_copy(
                    cv_hbm.at[layer, page, :, h_idx, :], vbuf.at[sl, dst, :], sem.at[1, sl]))
        return out

    @pl.when(step == 0)
    def _():
        for ahead in range(GATHER_SLOTS - 1):
            for cp in copies(ahead):
                cp.start()

    @pl.when(step + GATHER_SLOTS - 1 < n_steps)
    def _():
        for cp in copies(step + GATHER_SLOTS - 1):
            cp.start()

    def rb_get(k):
        return rb_ref[h * N_BUCKETS + k] * LOG2E

    q = q_ref[0, 0:t_len, :] * (HEAD_DIM ** -0.5 * LOG2E)
    kn = kn_ref[0, 0:t_len, :]
    vn = vn_ref[0, 0:t_len, :]
    n_cols = n_sel * MOBA_BLOCK
    col = lax.broadcasted_iota(I32, (t_len, n_cols), 1)
    row = lax.broadcasted_iota(I32, (t_len, n_cols), 0)
    far = jnp.broadcast_to(jnp.concatenate([far_ref[0]] * (MOBA_BLOCK // LANES), axis=1),
                           (t_len, MOBA_BLOCK))
    last = last_ref[0, 0:t_len, :]
    last_block = past_len // MOBA_BLOCK - 1
    bias = jnp.concatenate(
        [jnp.where(sel_at(step, j) == last_block, last, far) for j in range(n_sel)], axis=1)
    per_q = MOBA_TOPK * MOBA_BLOCK
    mine = (col >= row * per_q) & (col < (row + 1) * per_q)
    bias = jnp.where(mine, bias, NEG)

    row1 = lax.broadcasted_iota(I32, (t_len, LANES), 0)
    s_own = []
    for j in range(t_len):
        d_o = row1 - j
        s_j = jnp.sum(q * kn[j:j + 1, :], axis=1, keepdims=True)
        s_own.append(jnp.where(d_o >= 0, s_j + _t5_bias(d_o, rb_get), NEG)[:, 0:1])

    for cp in copies(step):
        cp.wait()
    s = lax.dot_general(q.astype(BF16), kbuf[slot].astype(BF16), NT_DIMS,
                        preferred_element_type=F32) + bias
    m = jnp.max(s, axis=1, keepdims=True)
    for s_j in s_own:
        m = jnp.maximum(m, s_j)
    p = jnp.exp2(s - m)
    den = jnp.sum(p, axis=1, keepdims=True)
    acc = jnp.dot(p.astype(BF16), vbuf[slot].astype(BF16), preferred_element_type=F32)
    for j, s_j in enumerate(s_own):
        p_j = jnp.exp2(s_j - m)
        den = den + p_j
        acc = acc + p_j * vn[j:j + 1, :]
    o_ref[0] = jnp.zeros(o_ref.shape[1:], F32)
    o_ref[0, 0:t_len, :] = acc / den


def _sample_attn(q3, k3, v3, far, last, cache_k, cache_v, page_table, sel_flat, rb_flat, layer, t):
    db, rows, d = q3.shape
    past_len = page_table.shape[1] * PAGE_SIZE
    assert past_len % MOBA_BLOCK == 0 and t <= SUBLANES and db * N_HEADS >= GATHER_SLOTS
    n_cols = t * MOBA_TOPK * MOBA_BLOCK
    spec = pl.BlockSpec((1, rows, HEAD_DIM), lambda b, h, pt, sl: (b, 0, h))
    farspec = pl.BlockSpec((1, 1, LANES), lambda b, h, pt, sl: (h, 0, 0))
    lastspec = pl.BlockSpec((1, SUBLANES, MOBA_BLOCK), lambda b, h, pt, sl: (h, 0, 0))
    anyspec = pl.BlockSpec(memory_space=pl.ANY)
    return pl.pallas_call(
        functools.partial(_sample_attn_body, layer, past_len, t),
        out_shape=jax.ShapeDtypeStruct((db, rows, d), F32),
        grid_spec=pltpu.PrefetchScalarGridSpec(
            num_scalar_prefetch=2,
            grid=(db, N_HEADS),
            in_specs=[spec, spec, spec, farspec, lastspec,
                      pl.BlockSpec(memory_space=pltpu.SMEM), anyspec, anyspec],
            out_specs=spec,
            scratch_shapes=[pltpu.VMEM((GATHER_SLOTS, n_cols, HEAD_DIM), F32),
                            pltpu.VMEM((GATHER_SLOTS, n_cols, HEAD_DIM), F32),
                            pltpu.SemaphoreType.DMA((2, GATHER_SLOTS))]),
        compiler_params=_params(("arbitrary", "arbitrary")),
        name="sample_attn",
    )(page_table, sel_flat, q3, k3, v3, far, last, rb_flat, cache_k, cache_v)


SUB = 8


def _gla_body(q_ref, k_ref, v_ref, lf_ref, gs_ref, gn_ref, s0_ref, o_ref, sfin_ref,
              st_scr, b_scr):
    ci = pl.program_id(1)
    chunk = q_ref.shape[0]
    nsub = chunk // SUB

    @pl.when(ci == 0)
    def _():
        for h in range(HG_HEADS):
            st_scr[h] = s0_ref[0, h].T

    r_i = lax.broadcasted_iota(I32, (chunk, chunk), 0)
    c_i = lax.broadcasted_iota(I32, (chunk, chunk), 1)
    tri = jnp.where(r_i >= c_i, 1.0, 0.0).astype(F32)
    b_all = jnp.dot(tri, lf_ref[...], precision=lax.Precision.HIGHEST,
                    preferred_element_type=F32) * LOG2E
    b_scr[...] = b_all
    neg_inf = -jnp.inf
    t_sub = lax.broadcasted_iota(I32, (SUB, HG_DIM), 0)
    causal = [t_sub >= s for s in range(SUB)]

    def zeros(n):
        return [jnp.zeros((n, HG_DIM), F32)] if n else []

    def cols(h):
        return slice(h * HG_DIM, (h + 1) * HG_DIM)


    o_state, att = [], []
    for h in range(HG_HEADS):
        sl = cols(h)
        b, q, k = b_all[:, sl], q_ref[:, sl], k_ref[:, sl]
        o_state.append(lax.dot_general((q * jnp.exp2(b)).astype(BF16),
                                       st_scr[h].astype(BF16), NT_DIMS,
                                       preferred_element_type=F32))
        if nsub > 1:
            qs, ks = [], []
            for j in range(nsub - 1):
                lo, hi = SUB * j, SUB * (j + 1)
                e_j = b[hi - 1:hi, :]
                qs.append(jnp.concatenate(
                    zeros(hi) + [q[hi:] * jnp.exp2(b[hi:] - e_j)], axis=0))
                ks.append(jnp.concatenate(
                    zeros(lo) + [k[lo:hi] * jnp.exp2(e_j - b[lo:hi])] + zeros(chunk - hi), axis=0))
            att.append(lax.dot_general(jnp.concatenate(qs, axis=1).astype(BF16),
                                       jnp.concatenate(ks, axis=1).astype(BF16), NT_DIMS,
                                       preferred_element_type=F32))

    for h in range(HG_HEADS):
        sl = cols(h)
        b = b_all[:, sl]
        b_last = b[chunk - 1:chunk, :]
        ke = k_ref[:, sl] * jnp.exp2(b_last - b)
        st_scr[h] = st_scr[h] * jnp.exp2(b_last) + lax.dot_general(
            v_ref[:, sl].astype(BF16), ke.astype(BF16), TN_DIMS, preferred_element_type=F32)

    o_diag = []
    for h in range(HG_HEADS):
        sl = cols(h)
        diag = []
        for i in range(nsub):
            r0 = SUB * i
            b_i, q_i = b_all[r0:r0 + SUB, sl], q_ref[r0:r0 + SUB, sl]
            o_i = jnp.zeros((SUB, HG_DIM), F32)
            for s in range(SUB):
                r = r0 + s
                dec = jnp.exp2(jnp.where(causal[s], b_i - b_scr[r:r + 1, sl], neg_inf))
                a = jnp.sum(dec * q_i * k_ref[r:r + 1, sl], axis=1, keepdims=True)
                o_i = o_i + a * v_ref[r:r + 1, sl]
            diag.append(o_i)
        o_diag.append(jnp.concatenate(diag, axis=0) if nsub > 1 else diag[0])

    for h in range(HG_HEADS):
        sl = cols(h)
        o = o_state[h] + o_diag[h]
        if nsub > 1:
            o = o + jnp.dot(att[h].astype(BF16), v_ref[:, sl].astype(BF16),
                            preferred_element_type=F32)
        o = o * lax.rsqrt(jnp.mean(o * o, axis=-1, keepdims=True) + EPS)
        o_ref[:, sl] = (o * gn_ref[:, sl] * gs_ref[:, sl]).astype(o_ref.dtype)

    @pl.when(ci == pl.num_programs(1) - 1)
    def _():
        for h in range(HG_HEADS):
            sfin_ref[0, h] = st_scr[h].T


def _gla(q, k, v, lf, gs, g_norm, s0, chunk):
    m, d = q.shape
    nb = s0.shape[0]
    nc = m // nb // chunk
    row = pl.BlockSpec((chunk, d), lambda b, c: (b * nc + c, 0))
    state = pl.BlockSpec((1, HG_HEADS, HG_DIM, HG_DIM), lambda b, c: (b, 0, 0, 0))
    return pl.pallas_call(
        _gla_body,
        out_shape=[jax.ShapeDtypeStruct((m, d), BF16),
                   jax.ShapeDtypeStruct(s0.shape, F32)],
        grid=(nb, nc),
        in_specs=[row, row, row, row, row, pl.BlockSpec((1, d), lambda b, c: (0, 0)), state],
        out_specs=[row, state],
        scratch_shapes=[pltpu.VMEM((HG_HEADS, HG_DIM, HG_DIM), F32),
                        pltpu.VMEM((chunk, d), F32)],
        compiler_params=_params(("arbitrary", "arbitrary")),
        name="gla",
    )(q, k, v, lf, gs, g_norm.reshape(1, d), s0)


PROMPT_TM = 512
MLP_TM = 1024
MLP_TF = 1024
GLA_CHUNK = 64
GLA_SAMPLE_CHUNK = 16


def kernel(x_prompt, x_sample, cache_k, cache_v, state_hgrn, page_table, norm_mix, norm_ffn,
           norm_final, attn_w_qkv, attn_w_o, rel_bias, hg_w_in, hg_lb_param, hg_norm, hg_w_o,
           ffn_w1, ffn_w2):
    b, s, d = x_prompt.shape
    db, t, _ = x_sample.shape
    tp = GLA_SAMPLE_CHUNK
    mp, ms = b * s, db * tp
    xp = x_prompt.reshape(mp, d)
    xs = jnp.pad(x_sample, ((0, 0), (0, tp - t), (0, 0))).reshape(ms, d)
    rb_flat = rel_bias.T.reshape(-1)
    w_qkv = _to_bf16(attn_w_qkv)
    w_ao = _to_bf16(attn_w_o)
    w_in = _to_bf16(hg_w_in)
    w_ho = _to_bf16(hg_w_o)
    w1 = _to_bf16(ffn_w1)
    w2 = _to_bf16(ffn_w2)

    own, adj, far, last = _bias_tables(rb_flat)
    qp, kp, vp = _qkv_proj(xp, norm_mix[0], w_qkv[0], PROMPT_TM)
    op, blk_mean = _moba_prompt(qp.reshape(b, s, d), kp.reshape(b, s, d), vp.reshape(b, s, d),
                                own, adj, far, cache_k, page_table, 0)
    xp = _proj_mlp(xp, op.reshape(mp, d), w_ao[0], norm_ffn[0], w1, w2, 0, norm_final,
                   False, MLP_TM, MLP_TF)

    qs, ks, vs = _qkv_proj(xs, norm_mix[0], w_qkv[0], ms)
    q3 = qs.reshape(db, tp, d)
    sel = _sample_select(q3, blk_mean)
    sel_flat = sel[:, :, :t, :MOBA_TOPK].reshape(-1)
    os_ = _sample_attn(q3, ks.reshape(db, tp, d), vs.reshape(db, tp, d), far, last,
                       cache_k, cache_v, page_table, sel_flat, rb_flat, 0, t)
    xs = _proj_mlp(xs, os_.reshape(ms, d).astype(BF16), w_ao[0], norm_ffn[0], w1, w2, 0,
                   norm_final, False, ms, MLP_TF)

    hq, hk, hv, hlf, hgs = _hgrn_proj(xp, norm_mix[1], w_in[0], hg_lb_param, 1, PROMPT_TM)
    s0p = jnp.zeros((b, HG_HEADS, HG_DIM, HG_DIM), F32)
    hop, s_prompt = _gla(hq, hk, hv, hlf, hgs, hg_norm[0], s0p, GLA_CHUNK)
    y_prompt = _proj_mlp(xp, hop, w_ho[0], norm_ffn[1], w1, w2, 1, norm_final,
                         True, MLP_TM, MLP_TF)

    outs = _hgrn_proj(xs, norm_mix[1], w_in[0], hg_lb_param, 1, ms, (tp, t))
    hos, s_sample = _gla(*outs, hg_norm[0], state_hgrn[0], tp)
    y_sample = _proj_mlp(xs, hos, w_ho[0], norm_ffn[1], w1, w2, 1, norm_final,
                         True, ms, MLP_TF)

    hd = (N_HEADS, HEAD_DIM)
    return (y_prompt.reshape(b, s, d), y_sample.reshape(db, tp, d)[:, :t],
            kp.reshape(1, b, s, *hd), vp.reshape(1, b, s, *hd),
            ks.reshape(1, db, tp, *hd)[:, :, :t], vs.reshape(1, db, tp, *hd)[:, :, :t],
            s_prompt[None], s_sample[None])
```

```python
import functools
import math

import jax
import jax.numpy as jnp
from jax import lax
from jax.experimental import pallas as pl
from jax.experimental.pallas import tpu as pltpu

F32 = jnp.float32
BF16 = jnp.bfloat16
I32 = jnp.int32

N_HEADS = 8
HEAD_DIM = 128
MOBA_BLOCK = 256
MOBA_TOPK = 3
PAGE_SIZE = 128
N_BUCKETS = 32
MAX_DISTANCE = 128
MAX_EXACT = N_BUCKETS // 2
HG_HEADS = 8
HG_DIM = 128
EPS = 1e-6

LANES = 128
SUBLANES = 8
VMEM_LIMIT = 56 * 1024 * 1024

NEG = -1e30
LOG2E = math.log2(math.e)

NT_DIMS = (((1,), (1,)), ((), ()))
TN_DIMS = (((0,), (0,)), ((), ()))


def _params(sem):
    return pltpu.CompilerParams(dimension_semantics=sem, vmem_limit_bytes=VMEM_LIMIT)


def _rms(x, g):
    ms = jnp.mean(x * x, axis=-1, keepdims=True)
    return x * lax.rsqrt(ms + EPS) * g


def _t5_bias(dist, rb_get):
    n = jnp.maximum(dist, 0)
    nf = jnp.maximum(n, MAX_EXACT).astype(F32)
    large = MAX_EXACT + jnp.floor(jnp.log(nf / MAX_EXACT) / math.log(MAX_DISTANCE / MAX_EXACT)
                                  * (N_BUCKETS - MAX_EXACT)).astype(I32)
    large = jnp.minimum(large, N_BUCKETS - 1)
    bucket = jnp.where(n < MAX_EXACT, n, large)
    out = jnp.zeros(dist.shape, F32)
    for k in range(N_BUCKETS):
        out = jnp.where(bucket == k, rb_get(k), out)
    return out


CAST_BLOCK_BYTES = 4 * 1024 * 1024
CAST_BLOCK_COLS = 1024


def _cast_body(x_ref, o_ref):
    o_ref[...] = x_ref[...].astype(o_ref.dtype)


def _to_bf16(w):
    cols = w.shape[-1]
    rows = math.prod(w.shape[:-1])
    bc = min(cols, CAST_BLOCK_COLS)
    budget_rows = max(2 * SUBLANES, CAST_BLOCK_BYTES // (bc * 4))
    br = min(rows, 1 << (budget_rows.bit_length() - 1))
    assert rows % br == 0 and cols % bc == 0
    spec = pl.BlockSpec((br, bc), lambda i, j: (i, j))
    out = pl.pallas_call(
        _cast_body,
        out_shape=jax.ShapeDtypeStruct((rows, cols), BF16),
        grid=(rows // br, cols // bc),
        in_specs=[spec],
        out_specs=spec,
        compiler_params=_params(("arbitrary", "arbitrary")),
        name="to_bf16",
    )(w.reshape(rows, cols))
    return out.reshape(w.shape)


def _qkv_body(x_ref, g_ref, w_ref, q_ref, k_ref, v_ref):
    d = x_ref.shape[1]
    xn = _rms(x_ref[...], g_ref[...]).astype(BF16)
    for c, o_ref in enumerate((q_ref, k_ref, v_ref)):
        o_ref[...] = jnp.dot(xn, w_ref[:, c * d:(c + 1) * d], preferred_element_type=F32)


def _qkv_proj(x, g, w_bf16, tm):
    m, d = x.shape
    row = pl.BlockSpec((tm, d), lambda i: (i, 0))
    return pl.pallas_call(
        _qkv_body,
        out_shape=[jax.ShapeDtypeStruct((m, d), F32)] * 3,
        grid=(m // tm,),
        in_specs=[row, pl.BlockSpec((1, d), lambda i: (0, 0)),
                  pl.BlockSpec((d, 3 * d), lambda i: (0, 0))],
        out_specs=[row] * 3,
        compiler_params=_params(("arbitrary",)),
        name="qkv_proj",
    )(x, g.reshape(1, d), w_bf16)


def _hgrn_proj_body(layer, seq_rows, x_ref, g_ref, w_ref, lbp_ref,
                    q_ref, k_ref, v_ref, lf_ref, gs_ref):
    d = x_ref.shape[1]
    xn = _rms(x_ref[...], g_ref[...]).astype(BF16)

    def proj(c):
        return jnp.dot(xn, w_ref[:, c * d:(c + 1) * d], preferred_element_type=F32)

    lbp = lbp_ref[...]
    e = jnp.exp(lbp - jnp.max(lbp, axis=0, keepdims=True))
    sm = e / jnp.sum(e, axis=0, keepdims=True)
    cum = sm[0:1]
    for r in range(1, layer + 1):
        cum = cum + sm[r:r + 1]
    lb = cum - sm[0:1]

    q_pre = proj(0)
    q_ref[...] = q_pre * jax.nn.sigmoid(q_pre) * (HG_DIM ** -0.5)
    fgate = lb + (1.0 - lb) * jax.nn.sigmoid(proj(1))
    if seq_rows is not None:
        row = lax.broadcasted_iota(I32, fgate.shape, 0)
        fgate = jnp.where((row & (seq_rows[0] - 1)) < seq_rows[1], fgate, 1.0)
    lf_ref[...] = jnp.log(fgate)
    k_ref[...] = 1.0 - fgate
    v_ref[...] = proj(2)
    g_pre = proj(3)
    gs_ref[...] = g_pre * jax.nn.sigmoid(g_pre)


def _hgrn_proj(x, g, w_bf16, lb_param, layer, tm, seq_rows=None):
    m, d = x.shape
    depth = lb_param.shape[0]
    assert seq_rows is None or (tm % seq_rows[0] == 0 and seq_rows[0] & (seq_rows[0] - 1) == 0)
    row = pl.BlockSpec((tm, d), lambda i: (i, 0))
    return pl.pallas_call(
        functools.partial(_hgrn_proj_body, layer, seq_rows),
        out_shape=[jax.ShapeDtypeStruct((m, d), F32)] * 5,
        grid=(m // tm,),
        in_specs=[row, pl.BlockSpec((1, d), lambda i: (0, 0)),
                  pl.BlockSpec((d, 4 * d), lambda i: (0, 0)),
                  pl.BlockSpec((depth, d), lambda i: (0, 0))],
        out_specs=[row] * 5,
        compiler_params=_params(("arbitrary",)),
        name="hgrn_proj",
    )(x, g.reshape(1, d), w_bf16, lb_param)


def _proj_mlp_body(final_norm, x_ref, o_ref, wo_ref, g_ref, w1_ref, w2_ref, gf_ref,
                   out_ref, x1_scr, xn_scr):
    f = pl.program_id(1)

    @pl.when(f == 0)
    def _():
        x1 = x_ref[...] + jnp.dot(o_ref[...], wo_ref[...], preferred_element_type=F32)
        x1_scr[...] = x1
        xn_scr[...] = _rms(x1, g_ref[...]).astype(BF16)
        out_ref[...] = jnp.zeros_like(out_ref)

    h = jnp.maximum(jnp.dot(xn_scr[...], w1_ref[...], preferred_element_type=F32), 0.0)
    out_ref[...] += jnp.dot((h * h).astype(BF16), w2_ref[...], preferred_element_type=F32)

    @pl.when(f == pl.num_programs(1) - 1)
    def _():
        y = x1_scr[...] + out_ref[...]
        if final_norm:
            y = _rms(y, gf_ref[...])
        out_ref[...] = y


def _proj_mlp(x, o_bf16, wo, g_ffn, w1_all, w2_all, layer, g_final, final_norm, tm, tf):
    m, d = x.shape
    dff = w1_all.shape[2]
    row = pl.BlockSpec((tm, d), lambda i, f: (i, 0))
    vec = pl.BlockSpec((1, d), lambda i, f: (0, 0))
    return pl.pallas_call(
        functools.partial(_proj_mlp_body, final_norm),
        out_shape=jax.ShapeDtypeStruct((m, d), F32),
        grid=(m // tm, dff // tf),
        in_specs=[row, row, pl.BlockSpec((d, d), lambda i, f: (0, 0)), vec,
                  pl.BlockSpec((None, d, tf), lambda i, f: (layer, 0, f)),
                  pl.BlockSpec((None, tf, d), lambda i, f: (layer, f, 0)), vec],
        out_specs=row,
        scratch_shapes=[pltpu.VMEM((tm, d), F32), pltpu.VMEM((tm, d), BF16)],
        compiler_params=_params(("arbitrary", "arbitrary")),
        name="proj_mlp",
    )(x, o_bf16, wo, g_ffn.reshape(1, d), w1_all, w2_all, g_final.reshape(1, d))


def _bias_tables_body(rb_ref, own_ref, adj_ref, far_ref, last_ref):
    h = pl.program_id(0)
    shape = (MOBA_BLOCK, MOBA_BLOCK)
    d_own = lax.broadcasted_iota(I32, shape, 1) - lax.broadcasted_iota(I32, shape, 0)

    def rb_get(k):
        return rb_ref[h * N_BUCKETS + k] * LOG2E

    own_ref[0] = jnp.where(d_own >= 0, _t5_bias(d_own, rb_get), NEG)
    adj_ref[0] = _t5_bias(d_own + MOBA_BLOCK, rb_get)
    d_far = jnp.full((1, LANES), MOBA_BLOCK + 1, I32)
    far_ref[0] = _t5_bias(d_far, rb_get)
    lshape = (SUBLANES, MOBA_BLOCK)
    d_last = (MOBA_BLOCK + lax.broadcasted_iota(I32, lshape, 0)
              - lax.broadcasted_iota(I32, lshape, 1))
    last_ref[0] = _t5_bias(d_last, rb_get)


def _bias_tables(rb_flat):
    tile = pl.BlockSpec((1, MOBA_BLOCK, MOBA_BLOCK), lambda h: (h, 0, 0))
    return pl.pallas_call(
        _bias_tables_body,
        out_shape=[jax.ShapeDtypeStruct((N_HEADS, MOBA_BLOCK, MOBA_BLOCK), F32),
                   jax.ShapeDtypeStruct((N_HEADS, MOBA_BLOCK, MOBA_BLOCK), F32),
                   jax.ShapeDtypeStruct((N_HEADS, 1, LANES), F32),
                   jax.ShapeDtypeStruct((N_HEADS, SUBLANES, MOBA_BLOCK), F32)],
        grid=(N_HEADS,),
        in_specs=[pl.BlockSpec(memory_space=pltpu.SMEM)],
        out_specs=[tile, tile, pl.BlockSpec((1, 1, LANES), lambda h: (h, 0, 0)),
                   pl.BlockSpec((1, SUBLANES, MOBA_BLOCK), lambda h: (h, 0, 0))],
        compiler_params=_params(("arbitrary",)),
        name="bias_tables",
    )(rb_flat)


STAGE_BLOCKS = 2
PAD_BLOCKS = 2 * STAGE_BLOCKS - 1
NB_MAX = 32
AUG = 2 * HEAD_DIM
FLAG_COL = 2 * NB_MAX
HEADS_PER_STEP = 4


MEAN_DEPTH = 8
MEAN_GROUP = 16
MEAN_PER_STEP = 2
MEAN_PER_GRID_STEP = 1
STAGE_SLOTS = 3


def _moba_prompt_body(layer, pt_ref, q_ref, k_ref, v_ref, own_ref, adj_ref, far_ref, ck_hbm,
                      o_ref, bm_hbm, kaug, vt, kmean, qaug, s_scr, p_scr, acc_s,
                      ring, stage, count, sem_in, sem_out):
    qi = pl.program_id(2)
    nb = kmean.shape[1]
    blk = MOBA_BLOCK
    heads = range(HEADS_PER_STEP)
    ppb = MOBA_BLOCK // PAGE_SIZE
    blocks_per_seq = pt_ref.shape[1] // ppb
    n_mean = pt_ref.shape[0] * blocks_per_seq
    grid_pos = (pl.program_id(0) * pl.num_programs(1) + pl.program_id(1)) * pl.num_programs(2) + qi
    n_steps = pl.num_programs(0) * pl.num_programs(1) * pl.num_programs(2)

    def hcols(i):
        return slice(i * HEAD_DIM, (i + 1) * HEAD_DIM)

    def page_copies(n):
        seq = n // blocks_per_seq
        first_page = (n - seq * blocks_per_seq) * ppb
        slot = n % MEAN_DEPTH
        return [pltpu.make_async_copy(ck_hbm.at[layer, pt_ref[seq, first_page + pg]],
                                      ring.at[slot, pg], sem_in.at[slot]) for pg in range(ppb)]

    def group_copy(g):
        slot = g % STAGE_SLOTS
        return pltpu.make_async_copy(
            stage.at[slot], bm_hbm.at[pl.ds(pl.multiple_of(g * MEAN_GROUP, MEAN_GROUP), MEAN_GROUP)],
            sem_out.at[slot])

    def reduce_block(n):
        slot = n % MEAN_DEPTH
        tot = jnp.sum(ring[slot, 0], axis=0)
        for pg in range(1, ppb):
            tot = tot + jnp.sum(ring[slot, pg], axis=0)
        g = n // MEAN_GROUP
        stage[jnp.where(n < n_mean, g % STAGE_SLOTS, STAGE_SLOTS), n - g * MEAN_GROUP] = (
            tot * (1.0 / MOBA_BLOCK))

    def advance_stream(n):
        @pl.when(n + MEAN_DEPTH < n_mean)
        def _():
            for cp in page_copies(n + MEAN_DEPTH):
                cp.start()

        @pl.when(n + MEAN_PER_STEP < n_mean)
        def _():
            for cp in page_copies(n + MEAN_PER_STEP):
                cp.wait()

        g = n // MEAN_GROUP

        @pl.when((n - g * MEAN_GROUP == MEAN_GROUP - 1) & (n < n_mean))
        def _():
            @pl.when(g >= 1)
            def _():
                group_copy(g - 1).wait()
            group_copy(g).start()

    @pl.when(grid_pos == 0)
    def _():
        count[0] = 0
        for n in range(MEAN_DEPTH):
            for cp in page_copies(n):
                cp.start()
        for n in range(MEAN_PER_STEP):
            for cp in page_copies(n):
                cp.wait()

    @pl.when(qi == 0)
    def _():
        lane_p = lax.broadcasted_iota(I32, (PAD_BLOCKS * blk, AUG), 1)
        pad_keys = jnp.where(lane_p == HEAD_DIM + FLAG_COL, 1.0, 0.0).astype(BF16)
        row_q = lax.broadcasted_iota(I32, (AUG - HEAD_DIM, blk), 0)
        for i in heads:
            kaug[i, 0:PAD_BLOCKS * blk, :] = pad_keys
            for n in range(PAD_BLOCKS):
                vt[i, n] = jnp.zeros((HEAD_DIM, blk), BF16)
            kmean[i] = jnp.zeros((nb, HEAD_DIM), F32)
            qaug[i, HEAD_DIM:AUG, :] = jnp.where(row_q == FLAG_COL, NEG, 0.0).astype(BF16)

    lane = lax.broadcasted_iota(I32, (blk, LANES), 1)
    hot = jnp.where(lane == qi, 1.0, jnp.where(lane == NB_MAX + qi, 1.0, 0.0)).astype(BF16)
    row_n = lax.broadcasted_iota(I32, (nb, HEAD_DIM), 0)
    own_rows = pl.ds(pl.multiple_of((qi + PAD_BLOCKS) * blk, blk), blk)
    for i in heads:
        kblk = k_ref[0, :, hcols(i)]
        kaug[i, own_rows, 0:HEAD_DIM] = kblk.astype(BF16)
        kaug[i, own_rows, HEAD_DIM:AUG] = hot
        vt[i, qi + PAD_BLOCKS] = v_ref[0, :, hcols(i)].T.astype(BF16)
        kmean[i] = jnp.where(row_n == qi, jnp.mean(kblk, axis=0, keepdims=True), kmean[i])

    qf = qi.astype(F32)

    gate = [lax.dot_general(kmean[i], q_ref[0, :, hcols(i)], NT_DIMS,
                            precision=lax.Precision.HIGHEST, preferred_element_type=F32)
            for i in heads]
    for i in heads:
        qaug[i, 0:HEAD_DIM, :] = (
            q_ref[0, :, hcols(i)] * (HEAD_DIM ** -0.5 * LOG2E)).T.astype(BF16)
    n_io = lax.broadcasted_iota(I32, gate[0].shape, 0).astype(F32)
    g = [jnp.where(n_io < qf, gate[i], -jnp.inf) for i in heads]
    sel = [jnp.zeros(gate[0].shape, F32) for i in heads]
    for _ in range(MOBA_TOPK):
        mx = [jnp.max(g[i], axis=0, keepdims=True) for i in heads]
        idx = [jnp.min(jnp.where(g[i] == mx[i], n_io, float(nb)), axis=0, keepdims=True)
               for i in heads]
        hit = [n_io == idx[i] for i in heads]
        sel = [jnp.where(hit[i], 1.0, sel[i]) for i in heads]
        g = [jnp.where(hit[i], -jnp.inf, g[i]) for i in heads]
    for i in heads:
        far = jnp.concatenate([far_ref[i]] * (blk // LANES), axis=1)
        picked = jnp.where(sel[i] > 0.0, jnp.where(n_io < qf - 1.0, far, 0.0), NEG)
        bias = jnp.where(n_io < qf, picked, jnp.where(n_io == qf, 0.0, NEG))
        bias_hi = bias.astype(BF16)
        qaug[i, HEAD_DIM:HEAD_DIM + nb, :] = bias_hi
        qaug[i, HEAD_DIM + NB_MAX:HEAD_DIM + NB_MAX + nb, :] = (
            bias - bias_hi.astype(F32)).astype(BF16)

    def stage_first(j):
        return qi - (STAGE_BLOCKS - 1) - STAGE_BLOCKS * j + PAD_BLOCKS

    def scores(i, first):
        r0 = pl.multiple_of(first * blk, blk)
        return jnp.dot(kaug[i, pl.ds(r0, STAGE_BLOCKS * blk), :], qaug[i],
                       preferred_element_type=F32)

    def values(i, first, p):
        out = jnp.dot(vt[i, first], p[0:blk], preferred_element_type=F32)
        for c in range(1, STAGE_BLOCKS):
            out = out + jnp.dot(vt[i, first + c], p[c * blk:(c + 1) * blk],
                                preferred_element_type=F32)
        return out

    for i in heads:
        s = scores(i, stage_first(0))
        far_rows = (STAGE_BLOCKS - 2) * blk
        s_scr[i] = jnp.concatenate(
            ([s[0:far_rows]] if far_rows else [])
            + [s[far_rows:far_rows + blk] + adj_ref[i], s[far_rows + blk:] + own_ref[i]], axis=0)
        acc_s[i] = jnp.zeros((HEAD_DIM, blk), F32)

    n0 = count[0]
    for c in range(MEAN_PER_GRID_STEP):
        reduce_block(n0 + c)
    carry0 = []
    for i in heads:
        s_next = scores(i, stage_first(1))
        s_cur = s_scr[i]
        m0 = jnp.max(s_cur, axis=0, keepdims=True)
        p = jnp.exp2(s_cur - m0)
        p_scr[i] = p.astype(BF16)
        s_scr[i] = s_next
        carry0 += [m0, jnp.sum(p, axis=0, keepdims=True), jnp.ones_like(m0)]
    for c in range(MEAN_PER_GRID_STEP):
        advance_stream(n0 + c)

    def step(t, carry):
        n = carry[-1]
        for c in range(MEAN_PER_STEP):
            reduce_block(n + c)
        out = []
        for i in heads:
            m_prev, l_prev, alpha_prev = carry[3 * i:3 * i + 3]
            s_next = scores(i, stage_first(t + 1))
            pv = values(i, stage_first(t - 1), p_scr[i])
            acc_s[i] = alpha_prev * acc_s[i] + pv
            s_cur = s_scr[i]
            m_new = jnp.maximum(m_prev, jnp.max(s_cur, axis=0, keepdims=True))
            alpha = jnp.exp2(m_prev - m_new)
            p = jnp.exp2(s_cur - m_new)
            p_scr[i] = p.astype(BF16)
            s_scr[i] = s_next
            out += [m_new, alpha * l_prev + jnp.sum(p, axis=0, keepdims=True), alpha]
        for c in range(MEAN_PER_STEP):
            advance_stream(n + c)
        return tuple(out) + (n + MEAN_PER_STEP,)

    n_stages = qi // STAGE_BLOCKS + 1
    fin = lax.fori_loop(1, n_stages, step, tuple(carry0) + (n0 + MEAN_PER_GRID_STEP,))
    for i in heads:
        pv = values(i, stage_first(n_stages - 1), p_scr[i])
        acc = fin[3 * i + 2] * acc_s[i] + pv
        o_ref[0, :, hcols(i)] = (acc / fin[3 * i + 1]).T.astype(o_ref.dtype)
    count[0] = fin[-1]

    @pl.when(grid_pos == n_steps - 1)
    def _():
        def drain(n, carry):
            reduce_block(n)
            advance_stream(n)
            return carry

        lax.fori_loop(fin[-1], n_mean, drain, 0)
        group_copy(n_mean // MEAN_GROUP - 1).wait()


def _moba_prompt(q, k, v, own, adj, far, cache_k, page_table, layer):
    b, s, d = q.shape
    nb = s // MOBA_BLOCK
    assert nb <= NB_MAX and FLAG_COL < AUG - HEAD_DIM and N_HEADS % HEADS_PER_STEP == 0
    blk = MOBA_BLOCK
    hp = HEADS_PER_STEP
    ppb = MOBA_BLOCK // PAGE_SIZE
    db, n_pages = page_table.shape
    n_mean = db * (n_pages // ppb)
    assert n_pages % ppb == 0 and n_mean % MEAN_GROUP == 0 and n_mean >= MEAN_DEPTH
    assert MEAN_DEPTH > MEAN_PER_STEP >= MEAN_PER_GRID_STEP and STAGE_BLOCKS >= 2
    qspec = pl.BlockSpec((1, blk, hp * HEAD_DIM), lambda bi, h, qi, pt: (bi, qi, h))
    kvspec = qspec
    tile = pl.BlockSpec((hp, blk, blk), lambda bi, h, qi, pt: (h, 0, 0))
    anyspec = pl.BlockSpec(memory_space=pl.ANY)
    out, means = pl.pallas_call(
        functools.partial(_moba_prompt_body, layer),
        out_shape=[jax.ShapeDtypeStruct((b, s, d), BF16),
                   jax.ShapeDtypeStruct((n_mean, N_HEADS, HEAD_DIM), F32)],
        grid_spec=pltpu.PrefetchScalarGridSpec(
            num_scalar_prefetch=1,
            grid=(b, N_HEADS // hp, nb),
            in_specs=[qspec, kvspec, kvspec, tile, tile,
                      pl.BlockSpec((hp, 1, LANES), lambda bi, h, qi, pt: (h, 0, 0)), anyspec],
            out_specs=[qspec, anyspec],
            scratch_shapes=[pltpu.VMEM((hp, (nb + PAD_BLOCKS) * blk, AUG), BF16),
                            pltpu.VMEM((hp, nb + PAD_BLOCKS, HEAD_DIM, blk), BF16),
                            pltpu.VMEM((hp, nb, HEAD_DIM), F32),
                            pltpu.VMEM((hp, AUG, blk), BF16),
                            pltpu.VMEM((hp, STAGE_BLOCKS * blk, blk), F32),
                            pltpu.VMEM((hp, STAGE_BLOCKS * blk, blk), BF16),
                            pltpu.VMEM((hp, HEAD_DIM, blk), F32),
                            pltpu.VMEM((MEAN_DEPTH, ppb, PAGE_SIZE, N_HEADS, HEAD_DIM), F32),
                            pltpu.VMEM((STAGE_SLOTS + 1, MEAN_GROUP, N_HEADS, HEAD_DIM), F32),
                            pltpu.SMEM((1,), I32),
                            pltpu.SemaphoreType.DMA((MEAN_DEPTH,)),
                            pltpu.SemaphoreType.DMA((STAGE_SLOTS,))]),
        compiler_params=_params(("arbitrary", "arbitrary", "arbitrary")),
        name="moba_prompt",
    )(page_table, q, k, v, own, adj, far, cache_k)
    return out, means.reshape(db, n_pages // ppb, N_HEADS, HEAD_DIM)


def _sample_select_body(q_ref, bm_ref, sel_ref):
    nblk = bm_ref.shape[1] // N_HEADS
    q = q_ref[0]
    for h in range(N_HEADS):
        qh = q[:, h * HEAD_DIM:(h + 1) * HEAD_DIM]
        bmh = bm_ref[0, pl.ds(h, nblk, stride=N_HEADS), :]
        g = lax.dot_general(qh, bmh, NT_DIMS, precision=lax.Precision.HIGHEST,
                            preferred_element_type=F32)
        n_iota = lax.broadcasted_iota(I32, g.shape, 1).astype(F32)
        lane = lax.broadcasted_iota(I32, (g.shape[0], LANES), 1)
        out = jnp.zeros((g.shape[0], LANES), I32)
        for r in range(MOBA_TOPK):
            mx = jnp.max(g, axis=1, keepdims=True)
            idx = jnp.min(jnp.where(g == mx, n_iota, float(nblk)), axis=1, keepdims=True)
            out = jnp.where(lane == r, idx.astype(I32), out)
            g = jnp.where(n_iota == idx, -jnp.inf, g)
        sel_ref[0, h] = out


def _sample_select(q3, blk_mean):
    db, t, d = q3.shape
    nblk = blk_mean.shape[1]
    bm2 = blk_mean.reshape(db, nblk * N_HEADS, HEAD_DIM)
    return pl.pallas_call(
        _sample_select_body,
        out_shape=jax.ShapeDtypeStruct((db, N_HEADS, t, LANES), I32),
        grid=(db,),
        in_specs=[pl.BlockSpec((1, t, d), lambda b: (b, 0, 0)),
                  pl.BlockSpec((1, nblk * N_HEADS, HEAD_DIM), lambda b: (b, 0, 0))],
        out_specs=pl.BlockSpec((1, N_HEADS, t, LANES), lambda b: (b, 0, 0, 0)),
        compiler_params=_params(("arbitrary",)),
        name="sample_select",
    )(q3, bm2)


GATHER_SLOTS = 4


def _sample_attn_body(layer, past_len, t_len, pt_ref, sel_ref, q_ref, kn_ref, vn_ref, far_ref,
                      last_ref, rb_ref, ck_hbm, cv_hbm, o_ref, kbuf, vbuf, sem):
    bi = pl.program_id(0)
    h = pl.program_id(1)
    nh = pl.num_programs(1)
    ppb = MOBA_BLOCK // PAGE_SIZE
    n_sel = t_len * MOBA_TOPK
    step = bi * nh + h
    n_steps = pl.num_programs(0) * nh
    slot = step % GATHER_SLOTS

    def sel_at(s_idx, j):
        return sel_ref[s_idx * n_sel + j]

    def page_copies(page, h_idx, sl, row0):
        dst = pl.ds(row0, PAGE_SIZE)
        return [pltpu.make_async_copy(ck_hbm.at[layer, page, :, h_idx, :], kbuf.at[sl, dst, :],
                                      sem.at[0, sl]),
                pltpu.make_async_copy(cv_hbm.at[layer, page, :, h_idx, :], vbuf.at[sl, dst, :],
                                      sem.at[1, sl])]

    def copies(s_idx):
        b_idx = s_idx // nh
        h_idx = s_idx - b_idx * nh
        sl = s_idx % GATHER_SLOTS
        out = []
        for j in range(n_sel):
            blk_id = sel_at(s_idx, j)
            for pg in range(ppb):
                out += page_copies(pt_ref[b_idx, blk_id * ppb + pg], h_idx, sl,
                                   j * MOBA_BLOCK + pg * PAGE_SIZE)
        return out

    def wait_slot(sl):
        for _ in range(n_sel * ppb):
            for cp in page_copies(0, 0, sl, 0):
                cp.wait()

    @pl.when(step == 0)
    def _():
        for ahead in range(GATHER_SLOTS - 1):
            for cp in copies(ahead):
                cp.start()

    @pl.when(step + GATHER_SLOTS - 1 < n_steps)
    def _():
        for cp in copies(step + GATHER_SLOTS - 1):
            cp.start()

    def rb_get(k):
        return rb_ref[h * N_BUCKETS + k] * LOG2E

    q = q_ref[0, 0:t_len, :] * (HEAD_DIM ** -0.5 * LOG2E)
    kn = kn_ref[0, 0:t_len, :]
    vn = vn_ref[0, 0:t_len, :]
    n_cols = n_sel * MOBA_BLOCK
    col = lax.broadcasted_iota(I32, (t_len, n_cols), 1)
    row = lax.broadcasted_iota(I32, (t_len, n_cols), 0)
    far = jnp.broadcast_to(jnp.concatenate([far_ref[0]] * (MOBA_BLOCK // LANES), axis=1),
                           (t_len, MOBA_BLOCK))
    last = last_ref[0, 0:t_len, :]
    last_block = past_len // MOBA_BLOCK - 1
    bias = jnp.concatenate(
        [jnp.where(sel_at(step, j) == last_block, last, far) for j in range(n_sel)], axis=1)
    per_q = MOBA_TOPK * MOBA_BLOCK
    mine = (col >= row * per_q) & (col < (row + 1) * per_q)
    bias = jnp.where(mine, bias, NEG)

    row1 = lax.broadcasted_iota(I32, (t_len, LANES), 0)
    s_own = []
    for j in range(t_len):
        d_o = row1 - j
        s_j = jnp.sum(q * kn[j:j + 1, :], axis=1, keepdims=True)
        s_own.append(jnp.where(d_o >= 0, s_j + _t5_bias(d_o, rb_get), NEG)[:, 0:1])

    wait_slot(slot)
    s = lax.dot_general(q.astype(BF16), kbuf[slot].astype(BF16), NT_DIMS,
                        preferred_element_type=F32) + bias
    m = jnp.max(s, axis=1, keepdims=True)
    for s_j in s_own:
        m = jnp.maximum(m, s_j)
    p = jnp.exp2(s - m)
    den = jnp.sum(p, axis=1, keepdims=True)
    acc = jnp.dot(p.astype(BF16), vbuf[slot].astype(BF16), preferred_element_type=F32)
    for j, s_j in enumerate(s_own):
        p_j = jnp.exp2(s_j - m)
        den = den + p_j
        acc = acc + p_j * vn[j:j + 1, :]
    o_ref[0] = jnp.zeros(o_ref.shape[1:], F32)
    o_ref[0, 0:t_len, :] = acc / den


def _sample_attn(q3, k3, v3, far, last, cache_k, cache_v, page_table, sel_flat, rb_flat, layer, t):
    db, rows, d = q3.shape
    past_len = page_table.shape[1] * PAGE_SIZE
    assert past_len % MOBA_BLOCK == 0 and t <= SUBLANES and db * N_HEADS >= GATHER_SLOTS
    n_cols = t * MOBA_TOPK * MOBA_BLOCK
    spec = pl.BlockSpec((1, rows, HEAD_DIM), lambda b, h, pt, sl: (b, 0, h))
    farspec = pl.BlockSpec((1, 1, LANES), lambda b, h, pt, sl: (h, 0, 0))
    lastspec = pl.BlockSpec((1, SUBLANES, MOBA_BLOCK), lambda b, h, pt, sl: (h, 0, 0))
    anyspec = pl.BlockSpec(memory_space=pl.ANY)
    return pl.pallas_call(
        functools.partial(_sample_attn_body, layer, past_len, t),
        out_shape=jax.ShapeDtypeStruct((db, rows, d), F32),
        grid_spec=pltpu.PrefetchScalarGridSpec(
            num_scalar_prefetch=2,
            grid=(db, N_HEADS),
            in_specs=[spec, spec, spec, farspec, lastspec,
                      pl.BlockSpec(memory_space=pltpu.SMEM), anyspec, anyspec],
            out_specs=spec,
            scratch_shapes=[pltpu.VMEM((GATHER_SLOTS, n_cols, HEAD_DIM), F32),
                            pltpu.VMEM((GATHER_SLOTS, n_cols, HEAD_DIM), F32),
                            pltpu.SemaphoreType.DMA((2, GATHER_SLOTS))]),
        compiler_params=_params(("arbitrary", "arbitrary")),
        name="sample_attn",
    )(page_table, sel_flat, q3, k3, v3, far, last, rb_flat, cache_k, cache_v)


SUB = 8


def _gla_body(valid_rows, q_ref, k_ref, v_ref, lf_ref, gs_ref, gn_ref, s0_ref, o_ref, sfin_ref,
              st_scr, b_scr):
    ci = pl.program_id(1)
    block_rows = q_ref.shape[0]
    chunk = block_rows if valid_rows is None else -(-valid_rows // SUB) * SUB
    nsub = chunk // SUB
    live = slice(0, chunk)

    @pl.when(ci == 0)
    def _():
        for h in range(HG_HEADS):
            st_scr[h] = s0_ref[0, h].T

    r_i = lax.broadcasted_iota(I32, (chunk, chunk), 0)
    c_i = lax.broadcasted_iota(I32, (chunk, chunk), 1)
    tri = jnp.where(r_i >= c_i, 1.0, 0.0).astype(F32)
    b_all = jnp.dot(tri, lf_ref[live, :], precision=lax.Precision.HIGHEST,
                    preferred_element_type=F32) * LOG2E
    b_scr[live, :] = b_all
    neg_inf = -jnp.inf
    t_sub = lax.broadcasted_iota(I32, (SUB, HG_DIM), 0)
    causal = [t_sub >= s for s in range(SUB)]

    def zeros(n):
        return [jnp.zeros((n, HG_DIM), F32)] if n else []

    def cols(h):
        return slice(h * HG_DIM, (h + 1) * HG_DIM)


    o_state, att = [], []
    for h in range(HG_HEADS):
        sl = cols(h)
        b, q, k = b_all[:, sl], q_ref[live, sl], k_ref[live, sl]
        o_state.append(lax.dot_general((q * jnp.exp2(b)).astype(BF16),
                                       st_scr[h].astype(BF16), NT_DIMS,
                                       preferred_element_type=F32))
        if nsub > 1:
            qs, ks = [], []
            for j in range(nsub - 1):
                lo, hi = SUB * j, SUB * (j + 1)
                e_j = b[hi - 1:hi, :]
                qs.append(jnp.concatenate(
                    zeros(hi) + [q[hi:] * jnp.exp2(b[hi:] - e_j)], axis=0))
                ks.append(jnp.concatenate(
                    zeros(lo) + [k[lo:hi] * jnp.exp2(e_j - b[lo:hi])] + zeros(chunk - hi), axis=0))
            att.append(lax.dot_general(jnp.concatenate(qs, axis=1).astype(BF16),
                                       jnp.concatenate(ks, axis=1).astype(BF16), NT_DIMS,
                                       preferred_element_type=F32))

    for h in range(HG_HEADS):
        sl = cols(h)
        b = b_all[:, sl]
        b_last = b[chunk - 1:chunk, :]
        ke = k_ref[live, sl] * jnp.exp2(b_last - b)
        st_scr[h] = st_scr[h] * jnp.exp2(b_last) + lax.dot_general(
            v_ref[live, sl].astype(BF16), ke.astype(BF16), TN_DIMS, preferred_element_type=F32)

    o_diag = []
    for h in range(HG_HEADS):
        sl = cols(h)
        diag = []
        for i in range(nsub):
            r0 = SUB * i
            b_i, q_i = b_all[r0:r0 + SUB, sl], q_ref[r0:r0 + SUB, sl]
            o_i = jnp.zeros((SUB, HG_DIM), F32)
            keys = SUB if valid_rows is None else min(SUB, valid_rows - r0)
            for s in range(keys):
                r = r0 + s
                dec = jnp.exp2(jnp.where(causal[s], b_i - b_scr[r:r + 1, sl], neg_inf))
                a = jnp.sum(dec * q_i * k_ref[r:r + 1, sl], axis=1, keepdims=True)
                o_i = o_i + a * v_ref[r:r + 1, sl]
            diag.append(o_i)
        o_diag.append(jnp.concatenate(diag, axis=0) if nsub > 1 else diag[0])

    for h in range(HG_HEADS):
        sl = cols(h)
        o = o_state[h] + o_diag[h]
        if nsub > 1:
            o = o + jnp.dot(att[h].astype(BF16), v_ref[live, sl].astype(BF16),
                            preferred_element_type=F32)
        o = o * lax.rsqrt(jnp.mean(o * o, axis=-1, keepdims=True) + EPS)
        o = o * gn_ref[:, sl] * gs_ref[live, sl]
        o_ref[:, sl] = jnp.concatenate([o] + zeros(block_rows - chunk), axis=0).astype(o_ref.dtype)

    @pl.when(ci == pl.num_programs(1) - 1)
    def _():
        for h in range(HG_HEADS):
            sfin_ref[0, h] = st_scr[h].T


def _gla(q, k, v, lf, gs, g_norm, s0, chunk, valid_rows=None):
    m, d = q.shape
    nb = s0.shape[0]
    nc = m // nb // chunk
    assert valid_rows is None or (0 < valid_rows <= chunk and nc == 1)
    row = pl.BlockSpec((chunk, d), lambda b, c: (b * nc + c, 0))
    state = pl.BlockSpec((1, HG_HEADS, HG_DIM, HG_DIM), lambda b, c: (b, 0, 0, 0))
    return pl.pallas_call(
        functools.partial(_gla_body, valid_rows),
        out_shape=[jax.ShapeDtypeStruct((m, d), BF16),
                   jax.ShapeDtypeStruct(s0.shape, F32)],
        grid=(nb, nc),
        in_specs=[row, row, row, row, row, pl.BlockSpec((1, d), lambda b, c: (0, 0)), state],
        out_specs=[row, state],
        scratch_shapes=[pltpu.VMEM((HG_HEADS, HG_DIM, HG_DIM), F32),
                        pltpu.VMEM((chunk, d), F32)],
        compiler_params=_params(("arbitrary", "arbitrary")),
        name="gla",
    )(q, k, v, lf, gs, g_norm.reshape(1, d), s0)


PROMPT_TM = 512
MLP_TM = 1024
MLP_TF = 1024
GLA_CHUNK = 64
GLA_SAMPLE_CHUNK = 16


def kernel(x_prompt, x_sample, cache_k, cache_v, state_hgrn, page_table, norm_mix, norm_ffn,
           norm_final, attn_w_qkv, attn_w_o, rel_bias, hg_w_in, hg_lb_param, hg_norm, hg_w_o,
           ffn_w1, ffn_w2):
    b, s, d = x_prompt.shape
    db, t, _ = x_sample.shape
    tp = GLA_SAMPLE_CHUNK
    mp, ms = b * s, db * tp
    xp = x_prompt.reshape(mp, d)
    xs = jnp.pad(x_sample, ((0, 0), (0, tp - t), (0, 0))).reshape(ms, d)
    rb_flat = rel_bias.T.reshape(-1)
    w_qkv = _to_bf16(attn_w_qkv)
    w_ao = _to_bf16(attn_w_o)
    w_in = _to_bf16(hg_w_in)
    w_ho = _to_bf16(hg_w_o)
    w1 = _to_bf16(ffn_w1)
    w2 = _to_bf16(ffn_w2)

    own, adj, far, last = _bias_tables(rb_flat)
    qp, kp, vp = _qkv_proj(xp, norm_mix[0], w_qkv[0], PROMPT_TM)
    op, blk_mean = _moba_prompt(qp.reshape(b, s, d), kp.reshape(b, s, d), vp.reshape(b, s, d),
                                own, adj, far, cache_k, page_table, 0)
    xp = _proj_mlp(xp, op.reshape(mp, d), w_ao[0], norm_ffn[0], w1, w2, 0, norm_final,
                   False, MLP_TM, MLP_TF)

    qs, ks, vs = _qkv_proj(xs, norm_mix[0], w_qkv[0], ms)
    q3 = qs.reshape(db, tp, d)
    sel = _sample_select(q3, blk_mean)
    sel_flat = sel[:, :, :t, :MOBA_TOPK].reshape(-1)
    os_ = _sample_attn(q3, ks.reshape(db, tp, d), vs.reshape(db, tp, d), far, last,
                       cache_k, cache_v, page_table, sel_flat, rb_flat, 0, t)
    xs = _proj_mlp(xs, os_.reshape(ms, d).astype(BF16), w_ao[0], norm_ffn[0], w1, w2, 0,
                   norm_final, False, ms, MLP_TF)

    hq, hk, hv, hlf, hgs = _hgrn_proj(xp, norm_mix[1], w_in[0], hg_lb_param, 1, PROMPT_TM)
    s0p = jnp.zeros((b, HG_HEADS, HG_DIM, HG_DIM), F32)
    hop, s_prompt = _gla(hq, hk, hv, hlf, hgs, hg_norm[0], s0p, GLA_CHUNK)
    y_prompt = _proj_mlp(xp, hop, w_ho[0], norm_ffn[1], w1, w2, 1, norm_final,
                         True, MLP_TM, MLP_TF)

    outs = _hgrn_proj(xs, norm_mix[1], w_in[0], hg_lb_param, 1, ms, (tp, t))
    hos, s_sample = _gla(*outs, hg_norm[0], state_hgrn[0], tp, t)
    y_sample = _proj_mlp(xs, hos, w_ho[0], norm_ffn[1], w1, w2, 1, norm_final,
                         True, ms, MLP_TF)

    hd = (N_HEADS, HEAD_DIM)
    return (y_prompt.reshape(b, s, d), y_sample.reshape(db, tp, d)[:, :t],
            kp.reshape(1, b, s, *hd), vp.reshape(1, b, s, *hd),
            ks.reshape(1, db, tp, *hd)[:, :, :t], vs.reshape(1, db, tp, *hd)[:, :, :t],
            s_prompt[None], s_sample[None])
```

```python
import functools
import math

import jax
import jax.numpy as jnp
from jax import lax
from jax.experimental import pallas as pl
from jax.experimental.pallas import tpu as pltpu

F32 = jnp.float32
BF16 = jnp.bfloat16
I32 = jnp.int32

N_HEADS = 8
HEAD_DIM = 128
MOBA_BLOCK = 256
MOBA_TOPK = 3
PAGE_SIZE = 128
N_BUCKETS = 32
MAX_DISTANCE = 128
MAX_EXACT = N_BUCKETS // 2
HG_HEADS = 8
HG_DIM = 128
EPS = 1e-6

LANES = 128
SUBLANES = 8
VMEM_LIMIT = 56 * 1024 * 1024

NEG = -1e30
LOG2E = math.log2(math.e)

NT_DIMS = (((1,), (1,)), ((), ()))
TN_DIMS = (((0,), (0,)), ((), ()))


def _params(sem):
    return pltpu.CompilerParams(dimension_semantics=sem, vmem_limit_bytes=VMEM_LIMIT)


def _rms(x, g):
    ms = jnp.mean(x * x, axis=-1, keepdims=True)
    return x * lax.rsqrt(ms + EPS) * g


def _t5_bias(dist, rb_get):
    n = jnp.maximum(dist, 0)
    nf = jnp.maximum(n, MAX_EXACT).astype(F32)
    large = MAX_EXACT + jnp.floor(jnp.log(nf / MAX_EXACT) / math.log(MAX_DISTANCE / MAX_EXACT)
                                  * (N_BUCKETS - MAX_EXACT)).astype(I32)
    large = jnp.minimum(large, N_BUCKETS - 1)
    bucket = jnp.where(n < MAX_EXACT, n, large)
    out = jnp.zeros(dist.shape, F32)
    for k in range(N_BUCKETS):
        out = jnp.where(bucket == k, rb_get(k), out)
    return out


CAST_BLOCK_BYTES = 4 * 1024 * 1024
CAST_BLOCK_COLS = 1024


def _cast_body(x_ref, o_ref):
    o_ref[...] = x_ref[...].astype(o_ref.dtype)


def _to_bf16(w):
    cols = w.shape[-1]
    rows = math.prod(w.shape[:-1])
    bc = min(cols, CAST_BLOCK_COLS)
    budget_rows = max(2 * SUBLANES, CAST_BLOCK_BYTES // (bc * 4))
    br = min(rows, 1 << (budget_rows.bit_length() - 1))
    assert rows % br == 0 and cols % bc == 0
    spec = pl.BlockSpec((br, bc), lambda i, j: (i, j))
    out = pl.pallas_call(
        _cast_body,
        out_shape=jax.ShapeDtypeStruct((rows, cols), BF16),
        grid=(rows // br, cols // bc),
        in_specs=[spec],
        out_specs=spec,
        compiler_params=_params(("arbitrary", "arbitrary")),
        name="to_bf16",
    )(w.reshape(rows, cols))
    return out.reshape(w.shape)


def _qkv_body(x_ref, g_ref, w_ref, q_ref, k_ref, v_ref):
    d = x_ref.shape[1]
    xn = _rms(x_ref[...], g_ref[...]).astype(BF16)
    for c, o_ref in enumerate((q_ref, k_ref, v_ref)):
        o_ref[...] = jnp.dot(xn, w_ref[:, c * d:(c + 1) * d], preferred_element_type=F32)


def _qkv_proj(x, g, w_bf16, tm):
    m, d = x.shape
    row = pl.BlockSpec((tm, d), lambda i: (i, 0))
    return pl.pallas_call(
        _qkv_body,
        out_shape=[jax.ShapeDtypeStruct((m, d), F32)] * 3,
        grid=(m // tm,),
        in_specs=[row, pl.BlockSpec((1, d), lambda i: (0, 0)),
                  pl.BlockSpec((d, 3 * d), lambda i: (0, 0))],
        out_specs=[row] * 3,
        compiler_params=_params(("arbitrary",)),
        name="qkv_proj",
    )(x, g.reshape(1, d), w_bf16)


def _hgrn_proj_body(layer, seq_rows, x_ref, g_ref, w_ref, lbp_ref,
                    q_ref, k_ref, v_ref, lf_ref, gs_ref):
    d = x_ref.shape[1]
    xn = _rms(x_ref[...], g_ref[...]).astype(BF16)

    def proj(c):
        return jnp.dot(xn, w_ref[:, c * d:(c + 1) * d], preferred_element_type=F32)

    lbp = lbp_ref[...]
    e = jnp.exp(lbp - jnp.max(lbp, axis=0, keepdims=True))
    sm = e / jnp.sum(e, axis=0, keepdims=True)
    cum = sm[0:1]
    for r in range(1, layer + 1):
        cum = cum + sm[r:r + 1]
    lb = cum - sm[0:1]

    q_pre = proj(0)
    q_ref[...] = q_pre * jax.nn.sigmoid(q_pre) * (HG_DIM ** -0.5)
    fgate = lb + (1.0 - lb) * jax.nn.sigmoid(proj(1))
    if seq_rows is not None:
        row = lax.broadcasted_iota(I32, fgate.shape, 0)
        fgate = jnp.where((row & (seq_rows[0] - 1)) < seq_rows[1], fgate, 1.0)
    lf_ref[...] = jnp.log(fgate)
    k_ref[...] = 1.0 - fgate
    v_ref[...] = proj(2)
    g_pre = proj(3)
    gs_ref[...] = g_pre * jax.nn.sigmoid(g_pre)


def _hgrn_proj(x, g, w_bf16, lb_param, layer, tm, seq_rows=None):
    m, d = x.shape
    depth = lb_param.shape[0]
    assert seq_rows is None or (tm % seq_rows[0] == 0 and seq_rows[0] & (seq_rows[0] - 1) == 0)
    row = pl.BlockSpec((tm, d), lambda i: (i, 0))
    return pl.pallas_call(
        functools.partial(_hgrn_proj_body, layer, seq_rows),
        out_shape=[jax.ShapeDtypeStruct((m, d), F32)] * 5,
        grid=(m // tm,),
        in_specs=[row, pl.BlockSpec((1, d), lambda i: (0, 0)),
                  pl.BlockSpec((d, 4 * d), lambda i: (0, 0)),
                  pl.BlockSpec((depth, d), lambda i: (0, 0))],
        out_specs=[row] * 5,
        compiler_params=_params(("arbitrary",)),
        name="hgrn_proj",
    )(x, g.reshape(1, d), w_bf16, lb_param)


def _proj_mlp_body(final_norm, x_ref, o_ref, wo_ref, g_ref, w1_ref, w2_ref, gf_ref,
                   out_ref, x1_scr, xn_scr):
    f = pl.program_id(1)

    @pl.when(f == 0)
    def _():
        x1 = x_ref[...] + jnp.dot(o_ref[...], wo_ref[...], preferred_element_type=F32)
        x1_scr[...] = x1
        xn_scr[...] = _rms(x1, g_ref[...]).astype(BF16)
        out_ref[...] = jnp.zeros_like(out_ref)

    h = jnp.maximum(jnp.dot(xn_scr[...], w1_ref[...], preferred_element_type=F32), 0.0)
    out_ref[...] += jnp.dot((h * h).astype(BF16), w2_ref[...], preferred_element_type=F32)

    @pl.when(f == pl.num_programs(1) - 1)
    def _():
        y = x1_scr[...] + out_ref[...]
        if final_norm:
            y = _rms(y, gf_ref[...])
        out_ref[...] = y


def _proj_mlp(x, o_bf16, wo, g_ffn, w1_all, w2_all, layer, g_final, final_norm, tm, tf):
    m, d = x.shape
    dff = w1_all.shape[2]
    row = pl.BlockSpec((tm, d), lambda i, f: (i, 0))
    vec = pl.BlockSpec((1, d), lambda i, f: (0, 0))
    return pl.pallas_call(
        functools.partial(_proj_mlp_body, final_norm),
        out_shape=jax.ShapeDtypeStruct((m, d), F32),
        grid=(m // tm, dff // tf),
        in_specs=[row, row, pl.BlockSpec((d, d), lambda i, f: (0, 0)), vec,
                  pl.BlockSpec((None, d, tf), lambda i, f: (layer, 0, f)),
                  pl.BlockSpec((None, tf, d), lambda i, f: (layer, f, 0)), vec],
        out_specs=row,
        scratch_shapes=[pltpu.VMEM((tm, d), F32), pltpu.VMEM((tm, d), BF16)],
        compiler_params=_params(("arbitrary", "arbitrary")),
        name="proj_mlp",
    )(x, o_bf16, wo, g_ffn.reshape(1, d), w1_all, w2_all, g_final.reshape(1, d))


def _bias_tables_body(rb_ref, own_ref, adj_ref, far_ref, last_ref):
    h = pl.program_id(0)
    shape = (MOBA_BLOCK, MOBA_BLOCK)
    d_own = lax.broadcasted_iota(I32, shape, 1) - lax.broadcasted_iota(I32, shape, 0)

    def rb_get(k):
        return rb_ref[h * N_BUCKETS + k] * LOG2E

    own_ref[0] = jnp.where(d_own >= 0, _t5_bias(d_own, rb_get), NEG)
    adj_ref[0] = _t5_bias(d_own + MOBA_BLOCK, rb_get)
    d_far = jnp.full((1, LANES), MOBA_BLOCK + 1, I32)
    far_ref[0] = _t5_bias(d_far, rb_get)
    lshape = (SUBLANES, MOBA_BLOCK)
    d_last = (MOBA_BLOCK + lax.broadcasted_iota(I32, lshape, 0)
              - lax.broadcasted_iota(I32, lshape, 1))
    last_ref[0] = _t5_bias(d_last, rb_get)


def _bias_tables(rb_flat):
    tile = pl.BlockSpec((1, MOBA_BLOCK, MOBA_BLOCK), lambda h: (h, 0, 0))
    return pl.pallas_call(
        _bias_tables_body,
        out_shape=[jax.ShapeDtypeStruct((N_HEADS, MOBA_BLOCK, MOBA_BLOCK), F32),
                   jax.ShapeDtypeStruct((N_HEADS, MOBA_BLOCK, MOBA_BLOCK), F32),
                   jax.ShapeDtypeStruct((N_HEADS, 1, LANES), F32),
                   jax.ShapeDtypeStruct((N_HEADS, SUBLANES, MOBA_BLOCK), F32)],
        grid=(N_HEADS,),
        in_specs=[pl.BlockSpec(memory_space=pltpu.SMEM)],
        out_specs=[tile, tile, pl.BlockSpec((1, 1, LANES), lambda h: (h, 0, 0)),
                   pl.BlockSpec((1, SUBLANES, MOBA_BLOCK), lambda h: (h, 0, 0))],
        compiler_params=_params(("arbitrary",)),
        name="bias_tables",
    )(rb_flat)


STAGE_BLOCKS = 2
PAD_BLOCKS = 2 * STAGE_BLOCKS - 1
NB_MAX = 32
AUG = 2 * HEAD_DIM
FLAG_COL = 2 * NB_MAX
HEADS_PER_STEP = 4


MEAN_DEPTH = 8
MEAN_GROUP = 16
MEAN_PER_STEP = 2
STAGE_SLOTS = 3


def _moba_prompt_body(layer, pt_ref, q_ref, k_ref, v_ref, own_ref, adj_ref, far_ref, ck_hbm,
                      o_ref, bm_hbm, kaug, vt, kmean, qaug, s_scr, p_scr, acc_s,
                      ring, stage, count, sem_in, sem_out):
    qi = pl.program_id(2)
    nb = kmean.shape[1]
    blk = MOBA_BLOCK
    heads = range(HEADS_PER_STEP)
    ppb = MOBA_BLOCK // PAGE_SIZE
    blocks_per_seq = pt_ref.shape[1] // ppb
    n_mean = pt_ref.shape[0] * blocks_per_seq
    grid_pos = (pl.program_id(0) * pl.num_programs(1) + pl.program_id(1)) * pl.num_programs(2) + qi
    n_steps = pl.num_programs(0) * pl.num_programs(1) * pl.num_programs(2)

    def hcols(i):
        return slice(i * HEAD_DIM, (i + 1) * HEAD_DIM)

    def page_copies(n):
        seq = n // blocks_per_seq
        first_page = (n - seq * blocks_per_seq) * ppb
        slot = n % MEAN_DEPTH
        return [pltpu.make_async_copy(ck_hbm.at[layer, pt_ref[seq, first_page + pg]],
                                      ring.at[slot, pg], sem_in.at[slot]) for pg in range(ppb)]

    def group_copy(g):
        slot = g % STAGE_SLOTS
        return pltpu.make_async_copy(
            stage.at[slot], bm_hbm.at[pl.ds(pl.multiple_of(g * MEAN_GROUP, MEAN_GROUP), MEAN_GROUP)],
            sem_out.at[slot])

    def reduce_block(n):
        slot = n % MEAN_DEPTH
        tot = jnp.sum(ring[slot, 0], axis=0)
        for pg in range(1, ppb):
            tot = tot + jnp.sum(ring[slot, pg], axis=0)
        g = n // MEAN_GROUP
        stage[jnp.where(n < n_mean, g % STAGE_SLOTS, STAGE_SLOTS), n - g * MEAN_GROUP] = (
            tot * (1.0 / MOBA_BLOCK))

    def advance_stream(n):
        @pl.when(n + MEAN_DEPTH < n_mean)
        def _():
            for cp in page_copies(n + MEAN_DEPTH):
                cp.start()

        @pl.when(n + MEAN_PER_STEP < n_mean)
        def _():
            for cp in page_copies(n + MEAN_PER_STEP):
                cp.wait()

        g = n // MEAN_GROUP

        @pl.when((n - g * MEAN_GROUP == MEAN_GROUP - 1) & (n < n_mean))
        def _():
            @pl.when(g >= 1)
            def _():
                group_copy(g - 1).wait()
            group_copy(g).start()

    @pl.when(grid_pos == 0)
    def _():
        count[0] = 0
        for n in range(MEAN_DEPTH):
            for cp in page_copies(n):
                cp.start()
        for n in range(MEAN_PER_STEP):
            for cp in page_copies(n):
                cp.wait()

    @pl.when(qi == 0)
    def _():
        lane_p = lax.broadcasted_iota(I32, (PAD_BLOCKS * blk, AUG), 1)
        pad_keys = jnp.where(lane_p == HEAD_DIM + FLAG_COL, 1.0, 0.0).astype(BF16)
        row_q = lax.broadcasted_iota(I32, (AUG - HEAD_DIM, blk), 0)
        for i in heads:
            kaug[i, 0:PAD_BLOCKS * blk, :] = pad_keys
            for n in range(PAD_BLOCKS):
                vt[i, n] = jnp.zeros((HEAD_DIM, blk), BF16)
            kmean[i] = jnp.zeros((nb, HEAD_DIM), F32)
            qaug[i, HEAD_DIM:AUG, :] = jnp.where(row_q == FLAG_COL, NEG, 0.0).astype(BF16)

    lane = lax.broadcasted_iota(I32, (blk, LANES), 1)
    hot = jnp.where(lane == qi, 1.0, jnp.where(lane == NB_MAX + qi, 1.0, 0.0)).astype(BF16)
    row_n = lax.broadcasted_iota(I32, (nb, HEAD_DIM), 0)
    own_rows = pl.ds(pl.multiple_of((qi + PAD_BLOCKS) * blk, blk), blk)
    for i in heads:
        kblk = k_ref[0, :, hcols(i)]
        kaug[i, own_rows, 0:HEAD_DIM] = kblk.astype(BF16)
        kaug[i, own_rows, HEAD_DIM:AUG] = hot
        vt[i, qi + PAD_BLOCKS] = v_ref[0, :, hcols(i)].T.astype(BF16)
        kmean[i] = jnp.where(row_n == qi, jnp.mean(kblk, axis=0, keepdims=True), kmean[i])

    qf = qi.astype(F32)

    gate = [lax.dot_general(kmean[i], q_ref[0, :, hcols(i)], NT_DIMS,
                            precision=lax.Precision.HIGHEST, preferred_element_type=F32)
            for i in heads]
    for i in heads:
        qaug[i, 0:HEAD_DIM, :] = (
            q_ref[0, :, hcols(i)] * (HEAD_DIM ** -0.5 * LOG2E)).T.astype(BF16)
    n_io = lax.broadcasted_iota(I32, gate[0].shape, 0).astype(F32)
    g = [jnp.where(n_io < qf, gate[i], -jnp.inf) for i in heads]
    sel = [jnp.zeros(gate[0].shape, F32) for i in heads]
    for _ in range(MOBA_TOPK):
        mx = [jnp.max(g[i], axis=0, keepdims=True) for i in heads]
        idx = [jnp.min(jnp.where(g[i] == mx[i], n_io, float(nb)), axis=0, keepdims=True)
               for i in heads]
        hit = [n_io == idx[i] for i in heads]
        sel = [jnp.where(hit[i], 1.0, sel[i]) for i in heads]
        g = [jnp.where(hit[i], -jnp.inf, g[i]) for i in heads]
    for i in heads:
        far = jnp.concatenate([far_ref[i]] * (blk // LANES), axis=1)
        picked = jnp.where(sel[i] > 0.0, jnp.where(n_io < qf - 1.0, far, 0.0), NEG)
        bias = jnp.where(n_io < qf, picked, jnp.where(n_io == qf, 0.0, NEG))
        bias_hi = bias.astype(BF16)
        qaug[i, HEAD_DIM:HEAD_DIM + nb, :] = bias_hi
        qaug[i, HEAD_DIM + NB_MAX:HEAD_DIM + NB_MAX + nb, :] = (
            bias - bias_hi.astype(F32)).astype(BF16)

    def stage_first(j):
        return qi - (STAGE_BLOCKS - 1) - STAGE_BLOCKS * j + PAD_BLOCKS

    def scores(i, first):
        r0 = pl.multiple_of(first * blk, blk)
        return jnp.dot(kaug[i, pl.ds(r0, STAGE_BLOCKS * blk), :], qaug[i],
                       preferred_element_type=F32)

    def values(i, first, p):
        out = jnp.dot(vt[i, first], p[0:blk], preferred_element_type=F32)
        for c in range(1, STAGE_BLOCKS):
            out = out + jnp.dot(vt[i, first + c], p[c * blk:(c + 1) * blk],
                                preferred_element_type=F32)
        return out

    for i in heads:
        s = scores(i, stage_first(0))
        s_scr[i] = jnp.concatenate(
            [s[0:blk] + adj_ref[i], s[blk:STAGE_BLOCKS * blk] + own_ref[i]], axis=0)
        acc_s[i] = jnp.zeros((HEAD_DIM, blk), F32)

    n0 = count[0]
    reduce_block(n0)
    carry0 = []
    for i in heads:
        s_next = scores(i, stage_first(1))
        s_cur = s_scr[i]
        m0 = jnp.max(s_cur, axis=0, keepdims=True)
        p = jnp.exp2(s_cur - m0)
        p_scr[i] = p.astype(BF16)
        s_scr[i] = s_next
        carry0 += [m0, jnp.sum(p, axis=0, keepdims=True), jnp.ones_like(m0)]
    advance_stream(n0)

    def step(t, carry):
        n = carry[-1]
        for c in range(MEAN_PER_STEP):
            reduce_block(n + c)
        out = []
        for i in heads:
            m_prev, l_prev, alpha_prev = carry[3 * i:3 * i + 3]
            s_next = scores(i, stage_first(t + 1))
            pv = values(i, stage_first(t - 1), p_scr[i])
            acc_s[i] = alpha_prev * acc_s[i] + pv
            s_cur = s_scr[i]
            m_new = jnp.maximum(m_prev, jnp.max(s_cur, axis=0, keepdims=True))
            alpha = jnp.exp2(m_prev - m_new)
            p = jnp.exp2(s_cur - m_new)
            p_scr[i] = p.astype(BF16)
            s_scr[i] = s_next
            out += [m_new, alpha * l_prev + jnp.sum(p, axis=0, keepdims=True), alpha]
        for c in range(MEAN_PER_STEP):
            advance_stream(n + c)
        return tuple(out) + (n + MEAN_PER_STEP,)

    n_stages = qi // STAGE_BLOCKS + 1
    fin = lax.fori_loop(1, n_stages, step, tuple(carry0) + (n0 + 1,))
    for i in heads:
        pv = values(i, stage_first(n_stages - 1), p_scr[i])
        acc = fin[3 * i + 2] * acc_s[i] + pv
        o_ref[0, :, hcols(i)] = (acc / fin[3 * i + 1]).T.astype(o_ref.dtype)
    count[0] = fin[-1]

    @pl.when(grid_pos == n_steps - 1)
    def _():
        def drain(n, carry):
            reduce_block(n)
            advance_stream(n)
            return carry

        lax.fori_loop(fin[-1], n_mean, drain, 0)
        group_copy(n_mean // MEAN_GROUP - 1).wait()


def _moba_prompt(q, k, v, own, adj, far, cache_k, page_table, layer):
    b, s, d = q.shape
    nb = s // MOBA_BLOCK
    assert nb <= NB_MAX and FLAG_COL < AUG - HEAD_DIM and N_HEADS % HEADS_PER_STEP == 0
    blk = MOBA_BLOCK
    hp = HEADS_PER_STEP
    ppb = MOBA_BLOCK // PAGE_SIZE
    db, n_pages = page_table.shape
    n_mean = db * (n_pages // ppb)
    assert n_pages % ppb == 0 and n_mean % MEAN_GROUP == 0 and n_mean >= MEAN_DEPTH
    assert MEAN_DEPTH > MEAN_PER_STEP and STAGE_BLOCKS == 2
    qspec = pl.BlockSpec((1, blk, hp * HEAD_DIM), lambda bi, h, qi, pt: (bi, qi, h))
    kvspec = qspec
    tile = pl.BlockSpec((hp, blk, blk), lambda bi, h, qi, pt: (h, 0, 0))
    anyspec = pl.BlockSpec(memory_space=pl.ANY)
    out, means = pl.pallas_call(
        functools.partial(_moba_prompt_body, layer),
        out_shape=[jax.ShapeDtypeStruct((b, s, d), BF16),
                   jax.ShapeDtypeStruct((n_mean, N_HEADS, HEAD_DIM), F32)],
        grid_spec=pltpu.PrefetchScalarGridSpec(
            num_scalar_prefetch=1,
            grid=(b, N_HEADS // hp, nb),
            in_specs=[qspec, kvspec, kvspec, tile, tile,
                      pl.BlockSpec((hp, 1, LANES), lambda bi, h, qi, pt: (h, 0, 0)), anyspec],
            out_specs=[qspec, anyspec],
            scratch_shapes=[pltpu.VMEM((hp, (nb + PAD_BLOCKS) * blk, AUG), BF16),
                            pltpu.VMEM((hp, nb + PAD_BLOCKS, HEAD_DIM, blk), BF16),
                            pltpu.VMEM((hp, nb, HEAD_DIM), F32),
                            pltpu.VMEM((hp, AUG, blk), BF16),
                            pltpu.VMEM((hp, STAGE_BLOCKS * blk, blk), F32),
                            pltpu.VMEM((hp, STAGE_BLOCKS * blk, blk), BF16),
                            pltpu.VMEM((hp, HEAD_DIM, blk), F32),
                            pltpu.VMEM((MEAN_DEPTH, ppb, PAGE_SIZE, N_HEADS, HEAD_DIM), F32),
                            pltpu.VMEM((STAGE_SLOTS + 1, MEAN_GROUP, N_HEADS, HEAD_DIM), F32),
                            pltpu.SMEM((1,), I32),
                            pltpu.SemaphoreType.DMA((MEAN_DEPTH,)),
                            pltpu.SemaphoreType.DMA((STAGE_SLOTS,))]),
        compiler_params=_params(("arbitrary", "arbitrary", "arbitrary")),
        name="moba_prompt",
    )(page_table, q, k, v, own, adj, far, cache_k)
    return out, means.reshape(db, n_pages // ppb, N_HEADS, HEAD_DIM)


def _sample_select_body(q_ref, bm_ref, sel_ref):
    nblk = bm_ref.shape[1] // N_HEADS
    q = q_ref[0]
    for h in range(N_HEADS):
        qh = q[:, h * HEAD_DIM:(h + 1) * HEAD_DIM]
        bmh = bm_ref[0, pl.ds(h, nblk, stride=N_HEADS), :]
        g = lax.dot_general(qh, bmh, NT_DIMS, precision=lax.Precision.HIGHEST,
                            preferred_element_type=F32)
        n_iota = lax.broadcasted_iota(I32, g.shape, 1).astype(F32)
        lane = lax.broadcasted_iota(I32, (g.shape[0], LANES), 1)
        out = jnp.zeros((g.shape[0], LANES), I32)
        for r in range(MOBA_TOPK):
            mx = jnp.max(g, axis=1, keepdims=True)
            idx = jnp.min(jnp.where(g == mx, n_iota, float(nblk)), axis=1, keepdims=True)
            out = jnp.where(lane == r, idx.astype(I32), out)
            g = jnp.where(n_iota == idx, -jnp.inf, g)
        sel_ref[0, h] = out


def _sample_select(q3, blk_mean):
    db, t, d = q3.shape
    nblk = blk_mean.shape[1]
    bm2 = blk_mean.reshape(db, nblk * N_HEADS, HEAD_DIM)
    return pl.pallas_call(
        _sample_select_body,
        out_shape=jax.ShapeDtypeStruct((db, N_HEADS, t, LANES), I32),
        grid=(db,),
        in_specs=[pl.BlockSpec((1, t, d), lambda b: (b, 0, 0)),
                  pl.BlockSpec((1, nblk * N_HEADS, HEAD_DIM), lambda b: (b, 0, 0))],
        out_specs=pl.BlockSpec((1, N_HEADS, t, LANES), lambda b: (b, 0, 0, 0)),
        compiler_params=_params(("arbitrary",)),
        name="sample_select",
    )(q3, bm2)


GATHER_SLOTS = 4


def _sample_attn_body(layer, past_len, t_len, pt_ref, sel_ref, q_ref, kn_ref, vn_ref, far_ref,
                      last_ref, rb_ref, ck_hbm, cv_hbm, o_ref, kbuf, vbuf, sem):
    bi = pl.program_id(0)
    h = pl.program_id(1)
    nh = pl.num_programs(1)
    ppb = MOBA_BLOCK // PAGE_SIZE
    n_sel = t_len * MOBA_TOPK
    step = bi * nh + h
    n_steps = pl.num_programs(0) * nh
    slot = step % GATHER_SLOTS

    def sel_at(s_idx, j):
        return sel_ref[s_idx * n_sel + j]

    def copies(s_idx):
        b_idx = s_idx // nh
        h_idx = s_idx - b_idx * nh
        sl = s_idx % GATHER_SLOTS
        out = []
        for j in range(n_sel):
            blk_id = sel_at(s_idx, j)
            for pg in range(ppb):
                page = pt_ref[b_idx, blk_id * ppb + pg]
                dst = pl.ds(j * MOBA_BLOCK + pg * PAGE_SIZE, PAGE_SIZE)
                out.append(pltpu.make_async_copy(
                    ck_hbm.at[layer, page, :, h_idx, :], kbuf.at[sl, dst, :], sem.at[0, sl]))
                out.append(pltpu.make_async_copy(
                    cv_hbm.at[layer, page, :, h_idx, :], vbuf.at[sl, dst, :], sem.at[1, sl]))
        return out

    @pl.when(step == 0)
    def _():
        for ahead in range(GATHER_SLOTS - 1):
            for c, cp in enumerate(copies(ahead)):
                cp.start(priority=c % 2)

    @pl.when(step + GATHER_SLOTS - 1 < n_steps)
    def _():
        for c, cp in enumerate(copies(step + GATHER_SLOTS - 1)):
            cp.start(priority=c % 2)

    def rb_get(k):
        return rb_ref[h * N_BUCKETS + k] * LOG2E

    q = q_ref[0, 0:t_len, :] * (HEAD_DIM ** -0.5 * LOG2E)
    kn = kn_ref[0, 0:t_len, :]
    vn = vn_ref[0, 0:t_len, :]
    n_cols = n_sel * MOBA_BLOCK
    col = lax.broadcasted_iota(I32, (t_len, n_cols), 1)
    row = lax.broadcasted_iota(I32, (t_len, n_cols), 0)
    far = jnp.broadcast_to(jnp.concatenate([far_ref[0]] * (MOBA_BLOCK // LANES), axis=1),
                           (t_len, MOBA_BLOCK))
    last = last_ref[0, 0:t_len, :]
    last_block = past_len // MOBA_BLOCK - 1
    bias = jnp.concatenate(
        [jnp.where(sel_at(step, j) == last_block, last, far) for j in range(n_sel)], axis=1)
    per_q = MOBA_TOPK * MOBA_BLOCK
    mine = (col >= row * per_q) & (col < (row + 1) * per_q)
    bias = jnp.where(mine, bias, NEG)

    row1 = lax.broadcasted_iota(I32, (t_len, LANES), 0)
    s_own = []
    for j in range(t_len):
        d_o = row1 - j
        s_j = jnp.sum(q * kn[j:j + 1, :], axis=1, keepdims=True)
        s_own.append(jnp.where(d_o >= 0, s_j + _t5_bias(d_o, rb_get), NEG)[:, 0:1])

    for cp in copies(step):
        cp.wait()
    s = lax.dot_general(q.astype(BF16), kbuf[slot].astype(BF16), NT_DIMS,
                        preferred_element_type=F32) + bias
    m = jnp.max(s, axis=1, keepdims=True)
    for s_j in s_own:
        m = jnp.maximum(m, s_j)
    p = jnp.exp2(s - m)
    den = jnp.sum(p, axis=1, keepdims=True)
    acc = jnp.dot(p.astype(BF16), vbuf[slot].astype(BF16), preferred_element_type=F32)
    for j, s_j in enumerate(s_own):
        p_j = jnp.exp2(s_j - m)
        den = den + p_j
        acc = acc + p_j * vn[j:j + 1, :]
    o_ref[0] = jnp.zeros(o_ref.shape[1:], F32)
    o_ref[0, 0:t_len, :] = acc / den


def _sample_attn(q3, k3, v3, far, last, cache_k, cache_v, page_table, sel_flat, rb_flat, layer, t):
    db, rows, d = q3.shape
    past_len = page_table.shape[1] * PAGE_SIZE
    assert past_len % MOBA_BLOCK == 0 and t <= SUBLANES and db * N_HEADS >= GATHER_SLOTS
    n_cols = t * MOBA_TOPK * MOBA_BLOCK
    spec = pl.BlockSpec((1, rows, HEAD_DIM), lambda b, h, pt, sl: (b, 0, h))
    farspec = pl.BlockSpec((1, 1, LANES), lambda b, h, pt, sl: (h, 0, 0))
    lastspec = pl.BlockSpec((1, SUBLANES, MOBA_BLOCK), lambda b, h, pt, sl: (h, 0, 0))
    anyspec = pl.BlockSpec(memory_space=pl.ANY)
    return pl.pallas_call(
        functools.partial(_sample_attn_body, layer, past_len, t),
        out_shape=jax.ShapeDtypeStruct((db, rows, d), F32),
        grid_spec=pltpu.PrefetchScalarGridSpec(
            num_scalar_prefetch=2,
            grid=(db, N_HEADS),
            in_specs=[spec, spec, spec, farspec, lastspec,
                      pl.BlockSpec(memory_space=pltpu.SMEM), anyspec, anyspec],
            out_specs=spec,
            scratch_shapes=[pltpu.VMEM((GATHER_SLOTS, n_cols, HEAD_DIM), F32),
                            pltpu.VMEM((GATHER_SLOTS, n_cols, HEAD_DIM), F32),
                            pltpu.SemaphoreType.DMA((2, GATHER_SLOTS))]),
        compiler_params=_params(("arbitrary", "arbitrary")),
        name="sample_attn",
    )(page_table, sel_flat, q3, k3, v3, far, last, rb_flat, cache_k, cache_v)


SUB = 8


def _gla_body(q_ref, k_ref, v_ref, lf_ref, gs_ref, gn_ref, s0_ref, o_ref, sfin_ref,
              st_scr, b_scr):
    ci = pl.program_id(1)
    chunk = q_ref.shape[0]
    nsub = chunk // SUB

    @pl.when(ci == 0)
    def _():
        for h in range(HG_HEADS):
            st_scr[h] = s0_ref[0, h].T

    r_i = lax.broadcasted_iota(I32, (chunk, chunk), 0)
    c_i = lax.broadcasted_iota(I32, (chunk, chunk), 1)
    tri = jnp.where(r_i >= c_i, 1.0, 0.0).astype(F32)
    b_all = jnp.dot(tri, lf_ref[...], precision=lax.Precision.HIGHEST,
                    preferred_element_type=F32) * LOG2E
    b_scr[...] = b_all
    neg_inf = -jnp.inf
    t_sub = lax.broadcasted_iota(I32, (SUB, HG_DIM), 0)
    causal = [t_sub >= s for s in range(SUB)]

    def zeros(n):
        return [jnp.zeros((n, HG_DIM), F32)] if n else []

    def cols(h):
        return slice(h * HG_DIM, (h + 1) * HG_DIM)


    o_state, att = [], []
    for h in range(HG_HEADS):
        sl = cols(h)
        b, q, k = b_all[:, sl], q_ref[:, sl], k_ref[:, sl]
        o_state.append(lax.dot_general((q * jnp.exp2(b)).astype(BF16),
                                       st_scr[h].astype(BF16), NT_DIMS,
                                       preferred_element_type=F32))
        if nsub > 1:
            qs, ks = [], []
            for j in range(nsub - 1):
                lo, hi = SUB * j, SUB * (j + 1)
                e_j = b[hi - 1:hi, :]
                qs.append(jnp.concatenate(
                    zeros(hi) + [q[hi:] * jnp.exp2(b[hi:] - e_j)], axis=0))
                ks.append(jnp.concatenate(
                    zeros(lo) + [k[lo:hi] * jnp.exp2(e_j - b[lo:hi])] + zeros(chunk - hi), axis=0))
            att.append(lax.dot_general(jnp.concatenate(qs, axis=1).astype(BF16),
                                       jnp.concatenate(ks, axis=1).astype(BF16), NT_DIMS,
                                       preferred_element_type=F32))

    for h in range(HG_HEADS):
        sl = cols(h)
        b = b_all[:, sl]
        b_last = b[chunk - 1:chunk, :]
        ke = k_ref[:, sl] * jnp.exp2(b_last - b)
        st_scr[h] = st_scr[h] * jnp.exp2(b_last) + lax.dot_general(
            v_ref[:, sl].astype(BF16), ke.astype(BF16), TN_DIMS, preferred_element_type=F32)

    o_diag = []
    for h in range(HG_HEADS):
        sl = cols(h)
        diag = []
        for i in range(nsub):
            r0 = SUB * i
            b_i, q_i = b_all[r0:r0 + SUB, sl], q_ref[r0:r0 + SUB, sl]
            o_i = jnp.zeros((SUB, HG_DIM), F32)
            for s in range(SUB):
                r = r0 + s
                dec = jnp.exp2(jnp.where(causal[s], b_i - b_scr[r:r + 1, sl], neg_inf))
                a = jnp.sum(dec * q_i * k_ref[r:r + 1, sl], axis=1, keepdims=True)
                o_i = o_i + a * v_ref[r:r + 1, sl]
            diag.append(o_i)
        o_diag.append(jnp.concatenate(diag, axis=0) if nsub > 1 else diag[0])

    for h in range(HG_HEADS):
        sl = cols(h)
        o = o_state[h] + o_diag[h]
        if nsub > 1:
            o = o + jnp.dot(att[h].astype(BF16), v_ref[:, sl].astype(BF16),
                            preferred_element_type=F32)
        o = o * lax.rsqrt(jnp.mean(o * o, axis=-1, keepdims=True) + EPS)
        o_ref[:, sl] = (o * gn_ref[:, sl] * gs_ref[:, sl]).astype(o_ref.dtype)

    @pl.when(ci == pl.num_programs(1) - 1)
    def _():
        for h in range(HG_HEADS):
            sfin_ref[0, h] = st_scr[h].T


def _gla(q, k, v, lf, gs, g_norm, s0, chunk):
    m, d = q.shape
    nb = s0.shape[0]
    nc = m // nb // chunk
    row = pl.BlockSpec((chunk, d), lambda b, c: (b * nc + c, 0))
    state = pl.BlockSpec((1, HG_HEADS, HG_DIM, HG_DIM), lambda b, c: (b, 0, 0, 0))
    return pl.pallas_call(
        _gla_body,
        out_shape=[jax.ShapeDtypeStruct((m, d), BF16),
                   jax.ShapeDtypeStruct(s0.shape, F32)],
        grid=(nb, nc),
        in_specs=[row, row, row, row, row, pl.BlockSpec((1, d), lambda b, c: (0, 0)), state],
        out_specs=[row, state],
        scratch_shapes=[pltpu.VMEM((HG_HEADS, HG_DIM, HG_DIM), F32),
                        pltpu.VMEM((chunk, d), F32)],
        compiler_params=_params(("arbitrary", "arbitrary")),
        name="gla",
    )(q, k, v, lf, gs, g_norm.reshape(1, d), s0)


PROMPT_TM = 512
MLP_TM = 1024
MLP_TF = 1024
GLA_CHUNK = 64
GLA_SAMPLE_CHUNK = 16


def kernel(x_prompt, x_sample, cache_k, cache_v, state_hgrn, page_table, norm_mix, norm_ffn,
           norm_final, attn_w_qkv, attn_w_o, rel_bias, hg_w_in, hg_lb_param, hg_norm, hg_w_o,
           ffn_w1, ffn_w2):
    b, s, d = x_prompt.shape
    db, t, _ = x_sample.shape
    tp = GLA_SAMPLE_CHUNK
    mp, ms = b * s, db * tp
    xp = x_prompt.reshape(mp, d)
    xs = jnp.pad(x_sample, ((0, 0), (0, tp - t), (0, 0))).reshape(ms, d)
    rb_flat = rel_bias.T.reshape(-1)
    w_qkv = _to_bf16(attn_w_qkv)
    w_ao = _to_bf16(attn_w_o)
    w_in = _to_bf16(hg_w_in)
    w_ho = _to_bf16(hg_w_o)
    w1 = _to_bf16(ffn_w1)
    w2 = _to_bf16(ffn_w2)

    own, adj, far, last = _bias_tables(rb_flat)
    qp, kp, vp = _qkv_proj(xp, norm_mix[0], w_qkv[0], PROMPT_TM)
    op, blk_mean = _moba_prompt(qp.reshape(b, s, d), kp.reshape(b, s, d), vp.reshape(b, s, d),
                                own, adj, far, cache_k, page_table, 0)
    xp = _proj_mlp(xp, op.reshape(mp, d), w_ao[0], norm_ffn[0], w1, w2, 0, norm_final,
                   False, MLP_TM, MLP_TF)

    qs, ks, vs = _qkv_proj(xs, norm_mix[0], w_qkv[0], ms)
    q3 = qs.reshape(db, tp, d)
    sel = _sample_select(q3, blk_mean)
    sel_flat = sel[:, :, :t, :MOBA_TOPK].reshape(-1)
    os_ = _sample_attn(q3, ks.reshape(db, tp, d), vs.reshape(db, tp, d), far, last,
                       cache_k, cache_v, page_table, sel_flat, rb_flat, 0, t)
    xs = _proj_mlp(xs, os_.reshape(ms, d).astype(BF16), w_ao[0], norm_ffn[0], w1, w2, 0,
                   norm_final, False, ms, MLP_TF)

    hq, hk, hv, hlf, hgs = _hgrn_proj(xp, norm_mix[1], w_in[0], hg_lb_param, 1, PROMPT_TM)
    s0p = jnp.zeros((b, HG_HEADS, HG_DIM, HG_DIM), F32)
    hop, s_prompt = _gla(hq, hk, hv, hlf, hgs, hg_norm[0], s0p, GLA_CHUNK)
    y_prompt = _proj_mlp(xp, hop, w_ho[0], norm_ffn[1], w1, w2, 1, norm_final,
                         True, MLP_TM, MLP_TF)

    outs = _hgrn_proj(xs, norm_mix[1], w_in[0], hg_lb_param, 1, ms, (tp, t))
    hos, s_sample = _gla(*outs, hg_norm[0], state_hgrn[0], tp)
    y_sample = _proj_mlp(xs, hos, w_ho[0], norm_ffn[1], w1, w2, 1, norm_final,
                         True, ms, MLP_TF)

    hd = (N_HEADS, HEAD_DIM)
    return (y_prompt.reshape(b, s, d), y_sample.reshape(db, tp, d)[:, :t],
            kp.reshape(1, b, s, *hd), vp.reshape(1, b, s, *hd),
            ks.reshape(1, db, tp, *hd)[:, :, :t], vs.reshape(1, db, tp, *hd)[:, :, :t],
            s_prompt[None], s_sample[None])
```
